```python
import jax, jax.numpy as jnp
from jax import lax
import numpy as np

D_MODEL = 4096
BATCH = 1
SEQ = 16384
DEPTH = 2

CHUNK = 64
Q_BLOCK = 128
MLA_HEADS = D_MODEL // 256
QK_NOPE = 128
QK_ROPE = 64
V_DIM = 128
Q_LORA = D_MODEL // 4
KV_LORA = 512
ROPE_THETA = 10000.0
POOL_WINDOWS = (2, 4, 8, 16)
POOL_CH = D_MODEL - MLA_HEADS * V_DIM
POOL_GROUP = POOL_CH // len(POOL_WINDOWS)
IN_COLS = Q_LORA + KV_LORA + QK_ROPE + POOL_CH
MIX_WIDTH = MLA_HEADS * V_DIM + POOL_CH
MEM_LEN = 256
X_HEADS = 4
X_HEAD_DIM = D_MODEL // 16
X_WIDTH = X_HEADS * X_HEAD_DIM
D_FF = ((8 * D_MODEL // 3 + 255) // 256) * 256
CONV_W = 3
EPS = 1e-6

kernel_name = "hybrid_mla_pool_memxattn_convffn"


def rmsnorm(x, g):
    x32 = x.astype(jnp.float32)
    y = x32 * lax.rsqrt(jnp.mean(x32 * x32, axis=-1, keepdims=True) + EPS)
    return (y * g.astype(jnp.float32)).astype(x.dtype)


def rope_tables(positions):
    inv = 1.0 / (ROPE_THETA ** (jnp.arange(0, QK_ROPE, 2, dtype=jnp.float32) / QK_ROPE))
    ang = positions.astype(jnp.float32)[..., None] * inv
    return jnp.cos(ang)[:, :, None, :], jnp.sin(ang)[:, :, None, :]


def apply_rope(t, cos, sin):
    t32 = t.astype(jnp.float32)
    t1, t2 = jnp.split(t32, 2, axis=-1)
    out = jnp.concatenate([t1 * cos - t2 * sin, t2 * cos + t1 * sin], axis=-1)
    return out.astype(t.dtype)


def block_causal_attention(q, k, v):
    B, S, H, Dqk = q.shape
    scale = Dqk ** -0.5
    k_chunk = jnp.arange(S) // CHUNK

    def one_block(bi):
        qs = bi * Q_BLOCK
        qb = lax.dynamic_slice_in_dim(q, qs, Q_BLOCK, axis=1)
        s = jnp.einsum('bqhd,bkhd->bhqk', qb, k).astype(jnp.float32) * scale
        q_chunk = (qs + jnp.arange(Q_BLOCK)) // CHUNK
        mask = k_chunk[None, :] <= q_chunk[:, None]
        p = jax.nn.softmax(jnp.where(mask, s, -jnp.inf), axis=-1)
        return jnp.einsum('bhqk,bkhd->bqhd', p.astype(v.dtype), v)

    out = lax.map(one_block, jnp.arange(S // Q_BLOCK))
    return jnp.transpose(out, (1, 0, 2, 3, 4)).reshape(B, S, H, v.shape[-1])


def mla_group(cq, ckv, kr, cos, sin, g_q, w_uq, g_kv, w_ukv):
    B, S, _ = cq.shape
    q = (rmsnorm(cq, g_q) @ w_uq).reshape(B, S, MLA_HEADS, QK_NOPE + QK_ROPE)
    q = jnp.concatenate([q[..., :QK_NOPE], apply_rope(q[..., QK_NOPE:], cos, sin)], axis=-1)
    kv = (rmsnorm(ckv, g_kv) @ w_ukv).reshape(B, S, MLA_HEADS, QK_NOPE + V_DIM)
    k_rope = apply_rope(kr[:, :, None, :], cos, sin)
    k = jnp.concatenate([kv[..., :QK_NOPE],
                         jnp.broadcast_to(k_rope, (B, S, MLA_HEADS, QK_ROPE))], axis=-1)
    v = kv[..., QK_NOPE:]
    return block_causal_attention(q, k, v).reshape(B, S, MLA_HEADS * V_DIM)


def pool_group(u, w_pool, s_pool):
    B, S, _ = u.shape
    u32 = u.astype(jnp.float32)
    csum = lax.cumsum(u32, axis=1)
    t = jnp.arange(S)
    outs = []
    for g, w in enumerate(POOL_WINDOWS):
        sl = slice(g * POOL_GROUP, (g + 1) * POOL_GROUP)
        cg = csum[..., sl]
        lag = jnp.pad(cg, ((0, 0), (w, 0), (0, 0)))[:, :S]
        cnt = jnp.minimum(t + 1, w).astype(jnp.float32)[None, :, None]
        outs.append((cg - lag) / cnt - u32[..., sl])
    p = jnp.stack(outs, axis=2).astype(u.dtype)
    y = jnp.einsum('bsgc,gcd->bsgd', p, w_pool).reshape(B, S, POOL_CH)
    return y * s_pool


def memory_cross_attention(h, mem_n, w_cq, w_ck, w_cv, w_co):
    B, S, _ = h.shape
    M = mem_n.shape[1]
    q = (h @ w_cq).reshape(B, S, X_HEADS, X_HEAD_DIM)
    k = (mem_n @ w_ck).reshape(B, M, X_HEADS, X_HEAD_DIM)
    v = (mem_n @ w_cv).reshape(B, M, X_HEADS, X_HEAD_DIM)
    s = jnp.einsum('bqhd,bkhd->bhqk', q, k).astype(jnp.float32) * (X_HEAD_DIM ** -0.5)
    p = jax.nn.softmax(s, axis=-1).astype(v.dtype)
    o = jnp.einsum('bhqk,bkhd->bqhd', p, v).reshape(B, S, X_WIDTH)
    return o @ w_co


def causal_dwconv(h, w, b):
    F = h.shape[-1]
    y = lax.conv_general_dilated(h, w[:, None, :], window_strides=(1,),
                                 padding=((CONV_W - 1, 0),),
                                 dimension_numbers=('NWC', 'WIO', 'NWC'),
                                 feature_group_count=F)
    return y + b


def setup_inputs(seed: int = 0) -> dict:
    key = jax.random.key(seed)
    ks = iter(jax.random.split(key, 32))

    def nrm(shape, fan_in):
        return jax.random.normal(next(ks), shape, jnp.float32) * (fan_in ** -0.5)

    def gain(shape):
        return 1.0 + 0.02 * jax.random.normal(next(ks), shape, jnp.float32)

    L = DEPTH
    x = jax.random.normal(next(ks), (BATCH, SEQ, D_MODEL), jnp.float32)
    mem = jax.random.normal(next(ks), (BATCH, MEM_LEN, D_MODEL), jnp.float32)
    start = jax.random.randint(next(ks), (BATCH, 1), 0, 4096, dtype=jnp.int32)
    positions = (start + jnp.arange(SEQ, dtype=jnp.int32)[None, :]).astype(jnp.int32)
    return {
        "x": x,
        "mem": mem,
        "positions": positions,
        "g_mix_pre": gain((L, D_MODEL)),
        "g_mix_post": gain((L, D_MODEL)),
        "w_in": nrm((L, D_MODEL, IN_COLS), D_MODEL),
        "g_q": gain((L, Q_LORA)),
        "w_uq": nrm((L, Q_LORA, MLA_HEADS * (QK_NOPE + QK_ROPE)), Q_LORA),
        "g_kv": gain((L, KV_LORA)),
        "w_ukv": nrm((L, KV_LORA, MLA_HEADS * (QK_NOPE + V_DIM)), KV_LORA),
        "w_pool": nrm((L, len(POOL_WINDOWS), POOL_GROUP, POOL_GROUP), POOL_GROUP),
        "s_pool": 1.0 + 0.1 * jax.random.normal(next(ks), (L, POOL_CH), jnp.float32),
        "w_out": nrm((L, MIX_WIDTH, D_MODEL), MIX_WIDTH),
        "g_x_pre": gain((L, D_MODEL)),
        "g_x_post": gain((L, D_MODEL)),
        "g_mem": gain((L, D_MODEL)),
        "w_cq": nrm((L, D_MODEL, X_WIDTH), D_MODEL),
        "w_ck": nrm((L, D_MODEL, X_WIDTH), D_MODEL),
        "w_cv": nrm((L, D_MODEL, X_WIDTH), D_MODEL),
        "w_co": nrm((L, X_WIDTH, D_MODEL), X_WIDTH),
        "g_ffn_pre": gain((L, D_MODEL)),
        "g_ffn_post": gain((L, D_MODEL)),
        "w_gate": nrm((L, D_MODEL, D_FF), D_MODEL),
        "w_up": nrm((L, D_MODEL, D_FF), D_MODEL),
        "conv_w": nrm((L, CONV_W, D_FF), CONV_W),
        "conv_b": 0.01 * jax.random.normal(next(ks), (L, D_FF), jnp.float32),
        "w_down": nrm((L, D_FF, D_MODEL), D_FF),
    }


def reference(x, mem, positions, g_mix_pre, g_mix_post, w_in, g_q, w_uq, g_kv, w_ukv,
              w_pool, s_pool, w_out, g_x_pre, g_x_post, g_mem, w_cq, w_ck, w_cv, w_co,
              g_ffn_pre, g_ffn_post, w_gate, w_up, conv_w, conv_b, w_down):
    cos, sin = rope_tables(positions)
    c1 = Q_LORA
    c2 = c1 + KV_LORA
    c3 = c2 + QK_ROPE
    for l in range(DEPTH):
        h = rmsnorm(x, g_mix_pre[l])
        z = h @ w_in[l]
        a = mla_group(z[..., :c1], z[..., c1:c2], z[..., c2:c3], cos, sin,
                      g_q[l], w_uq[l], g_kv[l], w_ukv[l])
        p = pool_group(z[..., c3:], w_pool[l], s_pool[l])
        m = jnp.concatenate([a, p], axis=-1) @ w_out[l]
        x = x + rmsnorm(m, g_mix_post[l])
        h = rmsnorm(x, g_x_pre[l])
        mem_n = rmsnorm(mem, g_mem[l])
        c = memory_cross_attention(h, mem_n, w_cq[l], w_ck[l], w_cv[l], w_co[l])
        x = x + rmsnorm(c, g_x_post[l])
        h = rmsnorm(x, g_ffn_pre[l])
        gate = causal_dwconv(h @ w_gate[l], conv_w[l], conv_b[l])
        f = (jax.nn.silu(gate) * (h @ w_up[l])) @ w_down[l]
        x = x + rmsnorm(f, g_ffn_post[l])
    return x
```

```python
import functools
import math

import jax
import jax.numpy as jnp
import numpy as np
from jax import lax
from jax.experimental import pallas as pl
from jax.experimental.pallas import tpu as pltpu

F32 = jnp.float32
BF16 = jnp.bfloat16

D_MODEL = 4096
SEQ = 16384
DEPTH = 2
CHUNK = 64
MLA_HEADS = 16
QK_NOPE = 128
QK_ROPE = 64
HALF_ROPE = QK_ROPE // 2
V_DIM = 128
Q_LORA = 1024
KV_LORA = 512
ROPE_THETA = 10000.0
POOL_WINDOWS = (2, 4, 8, 16)
POOL_CH = 2048
POOL_GROUP = 512
POOL_HALO = 16
MEM_LEN = 256
X_HEADS = 4
X_HEAD_DIM = 256
X_WIDTH = X_HEADS * X_HEAD_DIM
D_FF = 11008
D_FF_PAD = 11264
CONV_W = 3
CONV_HALO = 8
EPS = 1e-6

QK_PAD = 256
ROPE_LANES = 128
Z_A = Q_LORA + KV_LORA + ROPE_LANES

V7X_VMEM_LIMIT = 56 * 1024 * 1024


def _params(*sem):
    return pltpu.CompilerParams(dimension_semantics=sem, vmem_limit_bytes=V7X_VMEM_LIMIT)


def _const_spec(shape):
    return pl.BlockSpec(shape, lambda *_: (0,) * len(shape), pipeline_mode=pl.Buffered(1))


def _rms(x, g):
    return x * lax.rsqrt(jnp.mean(x * x, axis=-1, keepdims=True) + EPS) * g


def _dot(a, b):
    return jnp.dot(a, b, preferred_element_type=F32)


def _dot_nt(a, b):
    return lax.dot_general(a, b, (((1,), (1,)), ((), ())), preferred_element_type=F32)


def _rope_kernel(pc_ref, pr_ref, invr_ref, sgnr_ref, invc_ref, sgnc_ref,
                 cosk_ref, sink_ref, cost_ref, sint_ref):
    ang = pc_ref[...].astype(F32) * invr_ref[...]
    cosk_ref[...] = jnp.cos(ang)
    sink_ref[...] = jnp.sin(ang) * sgnr_ref[...]
    angt = invc_ref[...] * pr_ref[...].astype(F32)
    cost_ref[...] = jnp.cos(angt)
    sint_ref[...] = jnp.sin(angt) * sgnc_ref[...]


def _rope_tables(positions, bs=2048):
    inv = 1.0 / (ROPE_THETA ** (jnp.arange(0, QK_ROPE, 2, dtype=F32) / QK_ROPE))
    zero = jnp.zeros((HALF_ROPE,), F32)
    one = jnp.ones((HALF_ROPE,), F32)
    inv4 = jnp.concatenate([inv, zero, inv, zero])
    sgn4 = jnp.concatenate([-one, zero, one, zero])
    pc = positions.reshape(SEQ, 1)
    pr = positions.reshape(1, SEQ)
    nat = pl.BlockSpec((bs, ROPE_LANES), lambda i: (i, 0))
    tr = pl.BlockSpec((ROPE_LANES, bs), lambda i: (0, i))
    row = pl.BlockSpec((1, ROPE_LANES), lambda i: (0, 0))
    col = pl.BlockSpec((ROPE_LANES, 1), lambda i: (0, 0))
    return pl.pallas_call(
        _rope_kernel,
        grid=(SEQ // bs,),
        in_specs=[pl.BlockSpec((bs, 1), lambda i: (i, 0)),
                  pl.BlockSpec((1, bs), lambda i: (0, i)),
                  row, row, col, col],
        out_specs=[nat, nat, tr, tr],
        out_shape=[jax.ShapeDtypeStruct((SEQ, ROPE_LANES), F32)] * 2
        + [jax.ShapeDtypeStruct((ROPE_LANES, SEQ), F32)] * 2,
        compiler_params=_params("arbitrary"),
        name="rope_tables",
    )(pc, pr, inv4.reshape(1, -1), sgn4.reshape(1, -1), inv4.reshape(-1, 1), sgn4.reshape(-1, 1))


def _norm_kernel(x_ref, g_ref, o_ref):
    o_ref[...] = _rms(x_ref[...], g_ref[...]).astype(o_ref.dtype)


def _norm(x, g, bm=512):
    m, d = x.shape
    return pl.pallas_call(
        _norm_kernel,
        grid=(m // bm,),
        in_specs=[pl.BlockSpec((bm, d), lambda i: (i, 0)), pl.BlockSpec((1, d), lambda i: (0, 0))],
        out_specs=pl.BlockSpec((bm, d), lambda i: (i, 0)),
        out_shape=jax.ShapeDtypeStruct((m, d), BF16),
        compiler_params=_params("arbitrary"),
        name="norm",
    )(x, g)


def _mm_kernel(a_ref, w_ref, o_ref):
    o_ref[...] = _dot(a_ref[...], w_ref[...]).astype(o_ref.dtype)


def _mm(a, w, bm, out_dtype, name):
    m, k = a.shape
    n = w.shape[1]
    return pl.pallas_call(
        _mm_kernel,
        grid=(m // bm,),
        in_specs=[pl.BlockSpec((bm, k), lambda i: (i, 0)), _const_spec((k, n))],
        out_specs=pl.BlockSpec((bm, n), lambda i: (i, 0)),
        out_shape=jax.ShapeDtypeStruct((m, n), out_dtype),
        compiler_params=_params("arbitrary"),
        name=name,
    )(a, w)


def _norm_mm_kernel(a_ref, g_ref, w_ref, o_ref):
    a = _rms(a_ref[...], g_ref[...]).astype(BF16)
    o_ref[...] = _dot(a, w_ref[...]).astype(o_ref.dtype)


def _norm_mm(a, g, w, bn, name):
    m, k = a.shape
    n = w.shape[1]
    return pl.pallas_call(
        _norm_mm_kernel,
        grid=(n // bn,),
        in_specs=[pl.BlockSpec((m, k), lambda j: (0, 0)), pl.BlockSpec((1, k), lambda j: (0, 0)),
                  pl.BlockSpec((k, bn), lambda j: (0, j))],
        out_specs=pl.BlockSpec((m, bn), lambda j: (0, j)),
        out_shape=jax.ShapeDtypeStruct((m, n), BF16),
        compiler_params=_params("arbitrary"),
        name=name,
    )(a, g, w)


def _pool_kernel(h_ref, wu_ref, wp_ref, sp_ref, o_ref, buf_ref, carry_ref, *, bm):
    i = pl.program_id(0)
    u = _dot(h_ref[...], wu_ref[...])

    @pl.when(i == 0)
    def _():
        buf_ref[0:POOL_HALO, :] = jnp.zeros((POOL_HALO, POOL_CH), F32)

    @pl.when(i > 0)
    def _():
        buf_ref[0:POOL_HALO, :] = carry_ref[...]

    buf_ref[POOL_HALO:POOL_HALO + bm, :] = u
    carry_ref[...] = u[bm - POOL_HALO:bm, :]

    t1 = i * bm + lax.broadcasted_iota(jnp.int32, (bm, 1), 0) + 1
    for g, w in enumerate(POOL_WINDOWS):
        c0, c1 = g * POOL_GROUP, (g + 1) * POOL_GROUP
        ug = u[:, c0:c1]
        s = ug
        for k in range(1, w):
            s = s + buf_ref[POOL_HALO - k:POOL_HALO - k + bm, c0:c1]
        inv_cnt = 1.0 / jnp.minimum(t1, w).astype(F32)
        p = (s * inv_cnt - ug).astype(BF16)
        y = _dot(p, wp_ref[g]) * sp_ref[:, c0:c1]
        o_ref[:, c0:c1] = y.astype(o_ref.dtype)


def _pool(h, w_u, w_pool, s_pool, bm=512):
    m, d = h.shape
    return pl.pallas_call(
        functools.partial(_pool_kernel, bm=bm),
        grid=(m // bm,),
        in_specs=[pl.BlockSpec((bm, d), lambda i: (i, 0)),
                  _const_spec((d, POOL_CH)),
                  _const_spec((len(POOL_WINDOWS), POOL_GROUP, POOL_GROUP)),
                  _const_spec((1, POOL_CH))],
        out_specs=pl.BlockSpec((bm, POOL_CH), lambda i: (i, 0)),
        out_shape=jax.ShapeDtypeStruct((m, POOL_CH), BF16),
        scratch_shapes=[pltpu.VMEM((POOL_HALO + bm, POOL_CH), F32), pltpu.VMEM((POOL_HALO, POOL_CH), F32)],
        compiler_params=_params("arbitrary"),
        name="pool",
    )(h, w_u, w_pool, s_pool)


def _qproj_kernel(cq_ref, g_ref, w_ref, cos_ref, sin_ref, o_ref):
    cqn = _rms(cq_ref[...], g_ref[...]).astype(BF16)
    qt = _dot_nt(w_ref[...], cqn)
    cos = cos_ref[...]
    sin = sin_ref[...]
    half = ROPE_LANES // 2
    for h in range(MLA_HEADS):
        r0 = h * QK_PAD
        o_ref[r0:r0 + QK_NOPE, :] = qt[r0:r0 + QK_NOPE, :].astype(o_ref.dtype)
        blk = qt[r0 + QK_NOPE:r0 + QK_PAD, :]
        swp = jnp.concatenate([blk[half:, :], blk[:half, :]], axis=0)
        o_ref[r0 + QK_NOPE:r0 + QK_PAD, :] = (blk * cos + swp * sin).astype(o_ref.dtype)


def _qproj(z_a, g_q, w_uqt, cost, sint, bm=512):
    m = z_a.shape[0]
    rows = MLA_HEADS * QK_PAD
    return pl.pallas_call(
        _qproj_kernel,
        grid=(m // bm,),
        in_specs=[pl.BlockSpec((bm, Q_LORA), lambda i: (i, 0)),
                  _const_spec((1, Q_LORA)),
                  _const_spec((rows, Q_LORA)),
                  pl.BlockSpec((ROPE_LANES, bm), lambda i: (0, i)),
                  pl.BlockSpec((ROPE_LANES, bm), lambda i: (0, i))],
        out_specs=pl.BlockSpec((rows, bm), lambda i: (0, i)),
        out_shape=jax.ShapeDtypeStruct((rows, m), BF16),
        compiler_params=_params("arbitrary"),
        name="q_proj",
    )(z_a, g_q, w_uqt, cost, sint)


def _kvproj_kernel(ckv_ref, kr_ref, g_ref, wk_ref, wvt_ref, cos_ref, sin_ref, k_ref, vt_ref):
    ckvn = _rms(ckv_ref[...], g_ref[...]).astype(BF16)
    kn = _dot(ckvn, wk_ref[...])
    vt_ref[0] = _dot_nt(wvt_ref[...], ckvn).astype(vt_ref.dtype)
    kr = kr_ref[...]
    swp = pltpu.roll(kr, ROPE_LANES // 2, 1)
    krope = (kr * cos_ref[...] + swp * sin_ref[...]).astype(k_ref.dtype)
    for h in range(MLA_HEADS):
        k_ref[h, :, 0:QK_NOPE] = kn[:, h * QK_NOPE:(h + 1) * QK_NOPE].astype(k_ref.dtype)
        k_ref[h, :, QK_NOPE:QK_PAD] = krope


def _kvproj(z_a, g_kv, w_uk, w_uvt, cosk, sink, bm):
    m = z_a.shape[0]
    return pl.pallas_call(
        _kvproj_kernel,
        grid=(m // bm,),
        in_specs=[pl.BlockSpec((bm, KV_LORA), lambda i: (i, Q_LORA // KV_LORA)),
                  pl.BlockSpec((bm, ROPE_LANES), lambda i: (i, (Q_LORA + KV_LORA) // ROPE_LANES)),
                  _const_spec((1, KV_LORA)),
                  _const_spec((KV_LORA, MLA_HEADS * QK_NOPE)),
                  _const_spec((MLA_HEADS * V_DIM, KV_LORA)),
                  pl.BlockSpec((bm, ROPE_LANES), lambda i: (i, 0)),
                  pl.BlockSpec((bm, ROPE_LANES), lambda i: (i, 0))],
        out_specs=[pl.BlockSpec((MLA_HEADS, bm, QK_PAD), lambda i: (0, i, 0)),
                   pl.BlockSpec((1, MLA_HEADS * V_DIM, bm), lambda i: (i, 0, 0))],
        out_shape=[jax.ShapeDtypeStruct((MLA_HEADS, m, QK_PAD), BF16),
                   jax.ShapeDtypeStruct((m // bm, MLA_HEADS * V_DIM, bm), BF16)],
        compiler_params=_params("arbitrary"),
        name="kv_proj",
    )(z_a, z_a, g_kv, w_uk, w_uvt, cosk, sink)


def _attn_kernel(qt_ref, k_ref, vt_ref, o_ref, *, blk):
    i = pl.program_id(1)
    q = qt_ref[...]

    def step(j, carry, masked):
        m, l, acc = carry
        k = k_ref[0, pl.ds(pl.multiple_of(j * blk, blk), blk), :]
        s = _dot(k, q)
        if masked:
            kc = lax.broadcasted_iota(jnp.int32, (blk, blk), 0) // CHUNK
            qc = lax.broadcasted_iota(jnp.int32, (blk, blk), 1) // CHUNK
            s = jnp.where(kc <= qc, s, -jnp.inf)
        m_new = jnp.maximum(m, jnp.max(s, axis=0, keepdims=True))
        alpha = jnp.exp2(m - m_new)
        p = jnp.exp2(s - m_new)
        l = alpha * l + jnp.sum(p, axis=0, keepdims=True)
        acc = alpha * acc + _dot(vt_ref[j], p.astype(BF16))
        return m_new, l, acc

    init = (jnp.full((1, blk), -jnp.inf, F32), jnp.zeros((1, blk), F32), jnp.zeros((V_DIM, blk), F32))
    carry = lax.fori_loop(0, i, lambda j, c: step(j, c, False), init)
    _, l, acc = step(i, carry, True)
    o_ref[...] = (acc / l).T.astype(o_ref.dtype)


def _attention(qt, k, vt, blk):
    m = k.shape[1]
    nb = m // blk
    return pl.pallas_call(
        functools.partial(_attn_kernel, blk=blk),
        grid=(MLA_HEADS, nb),
        in_specs=[pl.BlockSpec((QK_PAD, blk), lambda h, i: (h, i)),
                  pl.BlockSpec((1, m, QK_PAD), lambda h, i: (h, 0, 0)),
                  pl.BlockSpec((nb, V_DIM, blk), lambda h, i: (0, h, 0))],
        out_specs=pl.BlockSpec((blk, V_DIM), lambda h, i: (i, h)),
        out_shape=jax.ShapeDtypeStruct((m, MLA_HEADS * V_DIM), BF16),
        compiler_params=_params("arbitrary", "arbitrary"),
        name="mla_attention",
    )(qt, k, vt)


def _residual_epilogue(acc, x_ref, gpost_ref, gnext_ref, xo_ref, ho_ref):
    xn = x_ref[...] + _rms(acc, gpost_ref[...])
    xo_ref[...] = xn
    if ho_ref is not None:
        ho_ref[...] = _rms(xn, gnext_ref[...]).astype(ho_ref.dtype)


def _mm_res_kernel(a1_ref, a2_ref, w_ref, x_ref, gpost_ref, gnext_ref, xo_ref, *rest, n1, nk):
    ho_ref = rest[0] if rest else None
    k = pl.program_id(1)

    @pl.when(k == 0)
    def _():
        xo_ref[...] = jnp.zeros_like(xo_ref)

    @pl.when(k < n1)
    def _():
        xo_ref[...] += _dot(a1_ref[...], w_ref[...])

    if n1 < nk:
        @pl.when(k >= n1)
        def _():
            xo_ref[...] += _dot(a2_ref[...], w_ref[...])

    @pl.when(k == nk - 1)
    def _():
        _residual_epilogue(xo_ref[...], x_ref, gpost_ref, gnext_ref, xo_ref, ho_ref)


def _mm_res(a1, a2, w, x, g_post, g_next, bm, bk, name):
    m, n = x.shape
    n1 = a1.shape[1] // bk
    nk = w.shape[0] // bk
    if a2 is None:
        a2 = a1
    has_next = g_next is not None
    if not has_next:
        g_next = g_post
    row = pl.BlockSpec((bm, n), lambda i, k: (i, 0))
    x_row = pl.BlockSpec((bm, n), lambda i, k: (i, 0), pipeline_mode=pl.Buffered(1))
    gain = _const_spec((1, n))
    out_specs = [row] + ([row] if has_next else [])
    out_shape = [jax.ShapeDtypeStruct((m, n), F32)] + ([jax.ShapeDtypeStruct((m, n), BF16)] if has_next else [])
    return pl.pallas_call(
        functools.partial(_mm_res_kernel, n1=n1, nk=nk),
        grid=(m // bm, nk),
        in_specs=[pl.BlockSpec((bm, bk), lambda i, k: (i, jnp.minimum(k, n1 - 1))),
                  pl.BlockSpec((bm, bk), lambda i, k: (i, jnp.maximum(k - n1, 0))),
                  pl.BlockSpec((bk, n), lambda i, k: (k, 0)),
                  x_row, gain, gain],
        out_specs=out_specs,
        out_shape=out_shape,
        compiler_params=_params("arbitrary", "arbitrary"),
        name=name,
    )(a1, a2, w, x, g_post, g_next)


def _xattn_kernel(h_ref, x_ref, wq_ref, kv_ref, wo_ref, gpost_ref, gnext_ref, xo_ref, ho_ref):
    q = _dot(h_ref[...], wq_ref[...]).astype(BF16)
    outs = []
    for hh in range(X_HEADS):
        c0, c1 = hh * X_HEAD_DIM, (hh + 1) * X_HEAD_DIM
        s = _dot_nt(q[:, c0:c1], kv_ref[:, c0:c1])
        p = jnp.exp(s - jnp.max(s, axis=-1, keepdims=True))
        l = jnp.sum(p, axis=-1, keepdims=True)
        o = _dot(p.astype(BF16), kv_ref[:, X_WIDTH + c0:X_WIDTH + c1]) / l
        outs.append(o.astype(BF16))
    c = _dot(jnp.concatenate(outs, axis=-1), wo_ref[...])
    _residual_epilogue(c, x_ref, gpost_ref, gnext_ref, xo_ref, ho_ref)


def _xattn(h, x, w_cq, kv, w_co, g_post, g_next, bm=256):
    m, d = x.shape
    row = pl.BlockSpec((bm, d), lambda i: (i, 0))
    gain = _const_spec((1, d))
    return pl.pallas_call(
        _xattn_kernel,
        grid=(m // bm,),
        in_specs=[row, row,
                  _const_spec((d, X_WIDTH)),
                  _const_spec((MEM_LEN, 2 * X_WIDTH)),
                  _const_spec((X_WIDTH, d)),
                  gain, gain],
        out_specs=[row, row],
        out_shape=[jax.ShapeDtypeStruct((m, d), F32), jax.ShapeDtypeStruct((m, d), BF16)],
        compiler_params=_params("arbitrary"),
        name="mem_xattn",
    )(h, x, w_cq, kv, w_co, g_post, g_next)


def _ffn_up_kernel(h_ref, wg_ref, wu_ref, cw_ref, cb_ref, o_ref, gbuf_ref, carry_ref, *, bm):
    i = pl.program_id(0)
    j = pl.program_id(1)
    h = h_ref[...]
    g = _dot(h, wg_ref[...])
    u = _dot(h, wu_ref[...])

    @pl.when(i == 0)
    def _():
        gbuf_ref[0:CONV_HALO, :] = jnp.zeros((CONV_HALO, g.shape[1]), F32)

    @pl.when(i > 0)
    def _():
        gbuf_ref[0:CONV_HALO, :] = carry_ref[j]

    gbuf_ref[CONV_HALO:CONV_HALO + bm, :] = g
    carry_ref[j] = g[bm - CONV_HALO:bm, :]

    y = cw_ref[2:3, :] * g + cb_ref[...]
    for tap in range(CONV_W - 1):
        lag = CONV_W - 1 - tap
        y = y + cw_ref[tap:tap + 1, :] * gbuf_ref[CONV_HALO - lag:CONV_HALO - lag + bm, :]
    act = y * (1.0 / (1.0 + jnp.exp(-y))) * u
    o_ref[...] = act.astype(o_ref.dtype)


def _ffn_up(h, w_gate, w_up, conv_w, conv_b, bm, bn):
    m, d = h.shape
    f = w_gate.shape[1]
    nj = f // bn
    return pl.pallas_call(
        functools.partial(_ffn_up_kernel, bm=bm),
        grid=(m // bm, nj),
        in_specs=[pl.BlockSpec((bm, d), lambda i, j: (i, 0)),
                  pl.BlockSpec((d, bn), lambda i, j: (0, j)),
                  pl.BlockSpec((d, bn), lambda i, j: (0, j)),
                  pl.BlockSpec((CONV_W, bn), lambda i, j: (0, j)),
                  pl.BlockSpec((1, bn), lambda i, j: (0, j))],
        out_specs=pl.BlockSpec((bm, bn), lambda i, j: (i, j)),
        out_shape=jax.ShapeDtypeStruct((m, f), BF16),
        scratch_shapes=[pltpu.VMEM((CONV_HALO + bm, bn), F32), pltpu.VMEM((nj, CONV_HALO, bn), F32)],
        compiler_params=_params("arbitrary", "arbitrary"),
        name="ffn_up",
    )(h, w_gate, w_up, conv_w, conv_b)


def _prep_layer(l, w_in, w_uq, w_ukv, w_pool, w_out, w_cq, w_ck, w_cv, w_co, w_gate, w_up, conv_w, conv_b, w_down):
    c1 = Q_LORA + KV_LORA
    c2 = c1 + QK_ROPE
    zr = jnp.zeros((D_MODEL, HALF_ROPE), F32)
    kr = w_in[l][:, c1:c2]
    w_a = jnp.concatenate([w_in[l][:, :c1], kr[:, :HALF_ROPE], zr, kr[:, HALF_ROPE:], zr], axis=1)

    qscale = (QK_NOPE + QK_ROPE) ** -0.5 * math.log2(math.e)
    wq = w_uq[l].reshape(Q_LORA, MLA_HEADS, QK_NOPE + QK_ROPE) * qscale
    zq = jnp.zeros((Q_LORA, MLA_HEADS, HALF_ROPE), F32)
    wq = jnp.concatenate([wq[..., :QK_NOPE], wq[..., QK_NOPE:QK_NOPE + HALF_ROPE], zq,
                          wq[..., QK_NOPE + HALF_ROPE:], zq], axis=-1)
    wkv = w_ukv[l].reshape(KV_LORA, MLA_HEADS, QK_NOPE + V_DIM)
    pad_f = D_FF_PAD - D_FF
    return dict(
        w_a=w_a.astype(BF16),
        w_u=w_in[l][:, c2:].astype(BF16),
        w_uqt=wq.reshape(Q_LORA, MLA_HEADS * QK_PAD).T.astype(BF16),
        w_uk=wkv[..., :QK_NOPE].reshape(KV_LORA, MLA_HEADS * QK_NOPE).astype(BF16),
        w_uvt=wkv[..., QK_NOPE:].reshape(KV_LORA, MLA_HEADS * V_DIM).T.astype(BF16),
        w_pool=w_pool[l].astype(BF16),
        w_out=w_out[l].astype(BF16),
        w_cq=(w_cq[l] * X_HEAD_DIM ** -0.5).astype(BF16),
        w_ckv=jnp.concatenate([w_ck[l], w_cv[l]], axis=1).astype(BF16),
        w_co=w_co[l].astype(BF16),
        w_gate=jnp.pad(w_gate[l], ((0, 0), (0, pad_f))).astype(BF16),
        w_up=jnp.pad(w_up[l], ((0, 0), (0, pad_f))).astype(BF16),
        conv_w=jnp.pad(conv_w[l], ((0, 0), (0, pad_f))),
        conv_b=jnp.pad(conv_b[l], ((0, pad_f),)).reshape(1, D_FF_PAD),
        w_down=jnp.pad(w_down[l], ((0, pad_f), (0, 0))).astype(BF16),
    )


ATT_BLK = 512


def kernel(x, mem, positions, g_mix_pre, g_mix_post, w_in, g_q, w_uq, g_kv, w_ukv, w_pool, s_pool, w_out,
           g_x_pre, g_x_post, g_mem, w_cq, w_ck, w_cv, w_co, g_ffn_pre, g_ffn_post, w_gate, w_up,
           conv_w, conv_b, w_down):
    x = x.reshape(SEQ, D_MODEL)
    mem2 = mem.reshape(MEM_LEN, D_MODEL)
    cosk, sink, cost, sint = _rope_tables(positions)
    h = _norm(x, g_mix_pre[0].reshape(1, -1))
    for l in range(DEPTH):
        p = _prep_layer(l, w_in, w_uq, w_ukv, w_pool, w_out, w_cq, w_ck, w_cv, w_co,
                        w_gate, w_up, conv_w, conv_b, w_down)
        z_a = _mm(h, p["w_a"], 512, F32, "in_proj")
        pooled = _pool(h, p["w_u"], p["w_pool"], s_pool[l].reshape(1, -1))
        qt = _qproj(z_a, g_q[l].reshape(1, -1), p["w_uqt"], cost, sint)
        k, vt = _kvproj(z_a, g_kv[l].reshape(1, -1), p["w_uk"], p["w_uvt"], cosk, sink, ATT_BLK)
        att = _attention(qt, k, vt, ATT_BLK)
        x, h = _mm_res(att, pooled, p["w_out"], x, g_mix_post[l].reshape(1, -1),
                       g_x_pre[l].reshape(1, -1), 512, 512, "out_proj")
        kv = _norm_mm(mem2, g_mem[l].reshape(1, -1), p["w_ckv"], 512, "mem_kv")
        x, h = _xattn(h, x, p["w_cq"], kv, p["w_co"], g_x_post[l].reshape(1, -1), g_ffn_pre[l].reshape(1, -1))
        act = _ffn_up(h, p["w_gate"], p["w_up"], p["conv_w"], p["conv_b"], 1024, 512)
        g_next = g_mix_pre[l + 1].reshape(1, -1) if l + 1 < DEPTH else None
        res = _mm_res(act, None, p["w_down"], x, g_ffn_post[l].reshape(1, -1), g_next, 512, 512, "ffn_down")
        x = res[0]
        h = res[1] if g_next is not None else None
    return x.reshape(1, SEQ, D_MODEL)
```

```python
import functools
import math

import jax
import jax.numpy as jnp
from jax import lax
from jax.experimental import pallas as pl
from jax.experimental.pallas import tpu as pltpu

F32 = jnp.float32
BF16 = jnp.bfloat16

D_MODEL = 4096
SEQ = 16384
DEPTH = 2
CHUNK = 64
MLA_HEADS = 16
QK_NOPE = 128
QK_ROPE = 64
HALF_ROPE = QK_ROPE // 2
V_DIM = 128
Q_LORA = 1024
KV_LORA = 512
ROPE_THETA = 10000.0
POOL_WINDOWS = (2, 4, 8, 16)
POOL_CH = 2048
POOL_GROUP = 512
POOL_HALO = 16
MEM_LEN = 256
X_HEADS = 4
X_HEAD_DIM = 256
X_WIDTH = X_HEADS * X_HEAD_DIM
D_FF = 11008
D_FF_PAD = 11264
CONV_W = 3
CONV_HALO = 8
EPS = 1e-6

QK_PAD = 256
ROPE_LANES = 128
Z_A = Q_LORA + KV_LORA + ROPE_LANES
ATT_BLK = 512

V7X_VMEM_LIMIT = 56 * 1024 * 1024


def _params(*sem):
    return pltpu.CompilerParams(dimension_semantics=sem, vmem_limit_bytes=V7X_VMEM_LIMIT)


def _const_spec(shape):
    return pl.BlockSpec(shape, lambda *_: (0,) * len(shape), pipeline_mode=pl.Buffered(1))


def _layer_spec(l, shape):
    return pl.BlockSpec((None,) + tuple(shape), lambda *_: (l,) + (0,) * len(shape),
                        pipeline_mode=pl.Buffered(1))


def _rms(x, g):
    return x * lax.rsqrt(jnp.mean(x * x, axis=-1, keepdims=True) + EPS) * g


def _dot(a, b):
    return jnp.dot(a, b, preferred_element_type=F32)


def _dot_nt(a, b):
    return lax.dot_general(a, b, (((1,), (1,)), ((), ())), preferred_element_type=F32)


def _rope_kernel(pc_ref, pr_ref, invr_ref, sgnr_ref, invc_ref, sgnc_ref,
                 cosk_ref, sink_ref, cost_ref, sint_ref):
    ang = pc_ref[...].astype(F32) * invr_ref[...]
    cosk_ref[...] = jnp.cos(ang)
    sink_ref[...] = jnp.sin(ang) * sgnr_ref[...]
    angt = invc_ref[...] * pr_ref[...].astype(F32)
    cost_ref[...] = jnp.cos(angt)
    sint_ref[...] = jnp.sin(angt) * sgnc_ref[...]


def _rope_tables(positions, bs=2048):
    inv = 1.0 / (ROPE_THETA ** (jnp.arange(0, QK_ROPE, 2, dtype=F32) / QK_ROPE))
    zero = jnp.zeros((HALF_ROPE,), F32)
    one = jnp.ones((HALF_ROPE,), F32)
    inv4 = jnp.concatenate([inv, zero, inv, zero])
    sgn4 = jnp.concatenate([-one, zero, one, zero])
    pc = positions.reshape(SEQ, 1)
    pr = positions.reshape(1, SEQ)
    nat = pl.BlockSpec((bs, ROPE_LANES), lambda i: (i, 0))
    tr = pl.BlockSpec((ROPE_LANES, bs), lambda i: (0, i))
    return pl.pallas_call(
        _rope_kernel,
        grid=(SEQ // bs,),
        in_specs=[pl.BlockSpec((bs, 1), lambda i: (i, 0)),
                  pl.BlockSpec((1, bs), lambda i: (0, i)),
                  _const_spec((1, ROPE_LANES)), _const_spec((1, ROPE_LANES)),
                  _const_spec((ROPE_LANES, 1)), _const_spec((ROPE_LANES, 1))],
        out_specs=[nat, nat, tr, tr],
        out_shape=[jax.ShapeDtypeStruct((SEQ, ROPE_LANES), F32)] * 2
        + [jax.ShapeDtypeStruct((ROPE_LANES, SEQ), F32)] * 2,
        compiler_params=_params("arbitrary"),
        name="rope_tables",
    )(pc, pr, inv4.reshape(1, -1), sgn4.reshape(1, -1), inv4.reshape(-1, 1), sgn4.reshape(-1, 1))


def _norm_kernel(x_ref, g_ref, o_ref):
    o_ref[...] = _rms(x_ref[...], g_ref[...]).astype(o_ref.dtype)


def _norm(x, g, l, bm=512):
    m, d = x.shape
    return pl.pallas_call(
        _norm_kernel,
        grid=(m // bm,),
        in_specs=[pl.BlockSpec((bm, d), lambda i: (i, 0)), _layer_spec(l, (1, d))],
        out_specs=pl.BlockSpec((bm, d), lambda i: (i, 0)),
        out_shape=jax.ShapeDtypeStruct((m, d), BF16),
        compiler_params=_params("arbitrary"),
        name="norm",
    )(x, g)


def _mm_kernel(a_ref, w_ref, o_ref):
    o_ref[...] = _dot(a_ref[...], w_ref[...]).astype(o_ref.dtype)


def _mm(a, w, l, bm, out_dtype, name):
    m, k = a.shape
    n = w.shape[-1]
    return pl.pallas_call(
        _mm_kernel,
        grid=(m // bm,),
        in_specs=[pl.BlockSpec((bm, k), lambda i: (i, 0)), _layer_spec(l, (k, n))],
        out_specs=pl.BlockSpec((bm, n), lambda i: (i, 0)),
        out_shape=jax.ShapeDtypeStruct((m, n), out_dtype),
        compiler_params=_params("arbitrary"),
        name=name,
    )(a, w)


def _norm_mm_kernel(a_ref, g_ref, w_ref, o_ref):
    a = _rms(a_ref[...], g_ref[...]).astype(BF16)
    o_ref[...] = _dot(a, w_ref[...]).astype(o_ref.dtype)


def _norm_mm(a, g, w, l, bn, name):
    m, k = a.shape
    n = w.shape[-1]
    return pl.pallas_call(
        _norm_mm_kernel,
        grid=(n // bn,),
        in_specs=[_const_spec((m, k)), _layer_spec(l, (1, k)),
                  pl.BlockSpec((None, k, bn), lambda j: (l, 0, j))],
        out_specs=pl.BlockSpec((m, bn), lambda j: (0, j)),
        out_shape=jax.ShapeDtypeStruct((m, n), BF16),
        compiler_params=_params("arbitrary"),
        name=name,
    )(a, g, w)


def _pool_kernel(h_ref, wu_ref, wp_ref, sp_ref, o_ref, buf_ref, carry_ref, *, bm):
    i = pl.program_id(0)

    @pl.when(i == 0)
    def _():
        buf_ref[0:POOL_HALO, :] = jnp.zeros((POOL_HALO, POOL_CH), F32)

    @pl.when(i > 0)
    def _():
        buf_ref[0:POOL_HALO, :] = carry_ref[...]

    h = h_ref[...]
    t1 = i * bm + lax.broadcasted_iota(jnp.int32, (bm, 1), 0) + 1
    for g, w in enumerate(POOL_WINDOWS):
        c0, c1 = g * POOL_GROUP, (g + 1) * POOL_GROUP
        ug = _dot(h, wu_ref[:, c0:c1])
        buf_ref[POOL_HALO:POOL_HALO + bm, c0:c1] = ug
        carry_ref[:, c0:c1] = ug[bm - POOL_HALO:bm, :]
        s = ug
        for k in range(1, w):
            s = s + buf_ref[POOL_HALO - k:POOL_HALO - k + bm, c0:c1]
        inv_cnt = 1.0 / jnp.minimum(t1, w).astype(F32)
        p = (s * inv_cnt - ug).astype(BF16)
        y = _dot(p, wp_ref[g]) * sp_ref[:, c0:c1]
        o_ref[:, c0:c1] = y.astype(o_ref.dtype)


def _pool(h, w_u, w_pool, s_pool, l, bm=512):
    m, d = h.shape
    return pl.pallas_call(
        functools.partial(_pool_kernel, bm=bm),
        grid=(m // bm,),
        in_specs=[pl.BlockSpec((bm, d), lambda i: (i, 0)),
                  _layer_spec(l, (d, POOL_CH)),
                  _layer_spec(l, (len(POOL_WINDOWS), POOL_GROUP, POOL_GROUP)),
                  _layer_spec(l, (1, POOL_CH))],
        out_specs=pl.BlockSpec((bm, POOL_CH), lambda i: (i, 0)),
        out_shape=jax.ShapeDtypeStruct((m, POOL_CH), BF16),
        scratch_shapes=[pltpu.VMEM((POOL_HALO + bm, POOL_CH), F32), pltpu.VMEM((POOL_HALO, POOL_CH), F32)],
        compiler_params=_params("arbitrary"),
        name="pool",
    )(h, w_u, w_pool, s_pool)


def _qproj_kernel(cq_ref, g_ref, w_ref, cos_ref, sin_ref, o_ref):
    cqn = _rms(cq_ref[...], g_ref[...]).astype(BF16)
    qt = _dot_nt(w_ref[...], cqn)
    cos = cos_ref[...]
    sin = sin_ref[...]
    half = ROPE_LANES // 2
    for h in range(MLA_HEADS):
        r0 = h * QK_PAD
        o_ref[r0:r0 + QK_NOPE, :] = qt[r0:r0 + QK_NOPE, :].astype(o_ref.dtype)
        blk = qt[r0 + QK_NOPE:r0 + QK_PAD, :]
        swp = jnp.concatenate([blk[half:, :], blk[:half, :]], axis=0)
        o_ref[r0 + QK_NOPE:r0 + QK_PAD, :] = (blk * cos + swp * sin).astype(o_ref.dtype)


def _qproj(z_a, g_q, w_uqt, cost, sint, l, bm=512):
    m = z_a.shape[0]
    rows = MLA_HEADS * QK_PAD
    return pl.pallas_call(
        _qproj_kernel,
        grid=(m // bm,),
        in_specs=[pl.BlockSpec((bm, Q_LORA), lambda i: (i, 0)),
                  _layer_spec(l, (1, Q_LORA)),
                  _layer_spec(l, (rows, Q_LORA)),
                  pl.BlockSpec((ROPE_LANES, bm), lambda i: (0, i)),
                  pl.BlockSpec((ROPE_LANES, bm), lambda i: (0, i))],
        out_specs=pl.BlockSpec((rows, bm), lambda i: (0, i)),
        out_shape=jax.ShapeDtypeStruct((rows, m), BF16),
        compiler_params=_params("arbitrary"),
        name="q_proj",
    )(z_a, g_q, w_uqt, cost, sint)


def _kvproj_kernel(ckv_ref, kr_ref, g_ref, wk_ref, wvt_ref, cos_ref, sin_ref, k_ref, vt_ref):
    ckvn = _rms(ckv_ref[...], g_ref[...]).astype(BF16)
    kn = _dot(ckvn, wk_ref[...])
    vt_ref[0] = _dot_nt(wvt_ref[...], ckvn).astype(vt_ref.dtype)
    kr = kr_ref[...]
    swp = pltpu.roll(kr, ROPE_LANES // 2, 1)
    krope = (kr * cos_ref[...] + swp * sin_ref[...]).astype(k_ref.dtype)
    for h in range(MLA_HEADS):
        k_ref[h, :, 0:QK_NOPE] = kn[:, h * QK_NOPE:(h + 1) * QK_NOPE].astype(k_ref.dtype)
        k_ref[h, :, QK_NOPE:QK_PAD] = krope


def _kvproj(z_a, g_kv, w_uk, w_uvt, cosk, sink, l, bm):
    m = z_a.shape[0]
    return pl.pallas_call(
        _kvproj_kernel,
        grid=(m // bm,),
        in_specs=[pl.BlockSpec((bm, KV_LORA), lambda i: (i, Q_LORA // KV_LORA)),
                  pl.BlockSpec((bm, ROPE_LANES), lambda i: (i, (Q_LORA + KV_LORA) // ROPE_LANES)),
                  _layer_spec(l, (1, KV_LORA)),
                  _layer_spec(l, (KV_LORA, MLA_HEADS * QK_NOPE)),
                  _layer_spec(l, (MLA_HEADS * V_DIM, KV_LORA)),
                  pl.BlockSpec((bm, ROPE_LANES), lambda i: (i, 0)),
                  pl.BlockSpec((bm, ROPE_LANES), lambda i: (i, 0))],
        out_specs=[pl.BlockSpec((MLA_HEADS, bm, QK_PAD), lambda i: (0, i, 0)),
                   pl.BlockSpec((1, MLA_HEADS * V_DIM, bm), lambda i: (i, 0, 0))],
        out_shape=[jax.ShapeDtypeStruct((MLA_HEADS, m, QK_PAD), BF16),
                   jax.ShapeDtypeStruct((m // bm, MLA_HEADS * V_DIM, bm), BF16)],
        compiler_params=_params("arbitrary"),
        name="kv_proj",
    )(z_a, z_a, g_kv, w_uk, w_uvt, cosk, sink)


def _attn_kernel(qt_ref, k_ref, vt_ref, o_ref, s0, s1, p0, p1, *, blk):
    i = pl.program_id(1)
    q = qt_ref[...]

    def scores(jn, s_w):
        off = pl.multiple_of(jn * blk, blk)
        s = _dot(k_ref[0, pl.ds(off, blk), :], q)
        s_w[...] = s
        return jnp.max(s, axis=0, keepdims=True)

    def probs(s, m, l, cmax):
        m_new = jnp.maximum(m, cmax)
        alpha = jnp.exp2(m - m_new)
        p = jnp.exp2(s - m_new)
        l = alpha * l + jnp.sum(p, axis=0, keepdims=True)
        return m_new, l, alpha, p.astype(BF16)

    def values(acc, alpha, p, jv):
        return alpha * acc + _dot(vt_ref[jv], p)

    def stage(jn, jv, s_w, s_r, p_w, p_r, carry):
        m, l, acc, alpha, cmax = carry
        cmax_n = scores(jn, s_w)
        acc = values(acc, alpha, p_r[...], jv)
        m, l, alpha, p = probs(s_r[...], m, l, cmax)
        p_w[...] = p
        return m, l, acc, alpha, cmax_n

    def finish(s_r, p_r, carry):
        m, l, acc, alpha, _ = carry
        acc = values(acc, alpha, p_r[...], jnp.maximum(i - 1, 0))
        qc = lax.broadcasted_iota(jnp.int32, (1, blk), 1) // CHUNK
        s = jnp.concatenate(
            [jnp.where(qc >= r, s_r[r * CHUNK:(r + 1) * CHUNK, :], -jnp.inf) for r in range(blk // CHUNK)],
            axis=0)
        m, l, alpha, p = probs(s, m, l, jnp.max(s, axis=0, keepdims=True))
        acc = values(acc, alpha, p, i)
        o_ref[...] = (acc / l).T.astype(o_ref.dtype)

    p1[...] = jnp.zeros_like(p1)
    cmax0 = scores(0, s0)
    carry = (jnp.full((1, blk), -jnp.inf, F32), jnp.zeros((1, blk), F32), jnp.zeros((V_DIM, blk), F32),
             jnp.ones((1, blk), F32), cmax0)

    def pair(t, carry):
        j = 2 * t
        carry = stage(j + 1, jnp.maximum(j - 1, 0), s1, s0, p0, p1, carry)
        return stage(j + 2, j, s0, s1, p1, p0, carry)

    carry = lax.fori_loop(0, i // 2, pair, carry)

    @pl.when(i % 2 == 1)
    def _():
        finish(s1, p0, stage(i, jnp.maximum(i - 2, 0), s1, s0, p0, p1, carry))

    @pl.when(i % 2 == 0)
    def _():
        finish(s0, p1, carry)


def _attention(qt, k, vt, blk):
    m = k.shape[1]
    nb = m // blk
    return pl.pallas_call(
        functools.partial(_attn_kernel, blk=blk),
        grid=(MLA_HEADS, nb),
        in_specs=[pl.BlockSpec((QK_PAD, blk), lambda h, i: (h, i)),
                  pl.BlockSpec((1, m, QK_PAD), lambda h, i: (h, 0, 0)),
                  pl.BlockSpec((nb, V_DIM, blk), lambda h, i: (0, h, 0))],
        out_specs=pl.BlockSpec((blk, V_DIM), lambda h, i: (i, h)),
        out_shape=jax.ShapeDtypeStruct((m, MLA_HEADS * V_DIM), BF16),
        scratch_shapes=[pltpu.VMEM((blk, blk), F32)] * 2 + [pltpu.VMEM((blk, blk), BF16)] * 2,
        compiler_params=_params("arbitrary", "arbitrary"),
        name="mla_attention",
    )(qt, k, vt)


def _residual_epilogue(acc, x_ref, gpost_ref, gnext_ref, xo_ref, ho_ref):
    xn = x_ref[...] + _rms(acc, gpost_ref[...])
    xo_ref[...] = xn
    if ho_ref is not None:
        ho_ref[...] = _rms(xn, gnext_ref[...]).astype(ho_ref.dtype)


def _mm_res_kernel(a1_ref, a2_ref, w_ref, x_ref, gpost_ref, gnext_ref, xo_ref, *rest, n1, nk, be, bc):
    ho_ref = rest[0] if len(rest) == 2 else None
    acc_ref = rest[-1]
    k = pl.program_id(1)
    n = acc_ref.shape[1]

    def accumulate(a_ref, first):
        a = a_ref[...]
        for c0 in range(0, n, bc):
            d = _dot(a, w_ref[:, c0:c0 + bc])
            if first:
                acc_ref[:, c0:c0 + bc] = d
            else:
                acc_ref[:, c0:c0 + bc] += d

    @pl.when(k == 0)
    def _():
        accumulate(a1_ref, True)

    if n1 > 1:
        @pl.when(jnp.logical_and(k > 0, k < n1))
        def _():
            accumulate(a1_ref, False)

    if n1 < nk:
        @pl.when(jnp.logical_and(k >= n1, k < nk))
        def _():
            accumulate(a2_ref, False)

    @pl.when(k >= nk)
    def _():
        r0 = pl.multiple_of((k - nk) * be, be)
        _residual_epilogue(acc_ref[pl.ds(r0, be), :], x_ref, gpost_ref, gnext_ref, xo_ref, ho_ref)


def _mm_res(a1, a2, w, x, g_post, g_next, l, l_next, bm, bk, be, name):
    m, n = x.shape
    n1 = a1.shape[1] // bk
    nk = w.shape[1] // bk
    ne = bm // be
    if a2 is None:
        a2 = a1
    n2 = max(nk - n1, 1)
    has_next = g_next is not None
    if not has_next:
        g_next, l_next = g_post, l
    rows = lambda i, k: (i * ne + jnp.maximum(k - nk, 0), 0)
    row = pl.BlockSpec((be, n), rows)
    out_specs = [row] + ([row] if has_next else [])
    out_shape = [jax.ShapeDtypeStruct((m, n), F32)] + ([jax.ShapeDtypeStruct((m, n), BF16)] if has_next else [])
    return pl.pallas_call(
        functools.partial(_mm_res_kernel, n1=n1, nk=nk, be=be, bc=512),
        grid=(m // bm, nk + ne),
        in_specs=[pl.BlockSpec((bm, bk), lambda i, k: (i, jnp.minimum(k, n1 - 1))),
                  pl.BlockSpec((bm, bk), lambda i, k: (i, jnp.clip(k - n1, 0, n2 - 1))),
                  pl.BlockSpec((None, bk, n), lambda i, k: (l, jnp.minimum(k, nk - 1), 0)),
                  row, _layer_spec(l, (1, n)), _layer_spec(l_next, (1, n))],
        out_specs=out_specs,
        out_shape=out_shape,
        scratch_shapes=[pltpu.VMEM((bm, n), F32)],
        compiler_params=_params("arbitrary", "arbitrary"),
        name=name,
    )(a1, a2, w, x, g_post, g_next)


def _xattn_kernel(h_ref, x_ref, wq_ref, kv_ref, wo_ref, gpost_ref, gnext_ref, xo_ref, ho_ref, *, br):
    for r0 in range(0, h_ref.shape[0], br):
        rows = slice(r0, r0 + br)
        q = _dot(h_ref[rows, :], wq_ref[...]).astype(BF16)
        outs = []
        for hh in range(X_HEADS):
            c0, c1 = hh * X_HEAD_DIM, (hh + 1) * X_HEAD_DIM
            s = _dot_nt(q[:, c0:c1], kv_ref[:, c0:c1])
            p = jnp.exp(s - jnp.max(s, axis=-1, keepdims=True))
            l = jnp.sum(p, axis=-1, keepdims=True)
            o = _dot(p.astype(BF16), kv_ref[:, X_WIDTH + c0:X_WIDTH + c1]) / l
            outs.append(o.astype(BF16))
        c = _dot(jnp.concatenate(outs, axis=-1), wo_ref[...])
        xn = x_ref[rows, :] + _rms(c, gpost_ref[...])
        xo_ref[rows, :] = xn
        ho_ref[rows, :] = _rms(xn, gnext_ref[...]).astype(ho_ref.dtype)


def _xattn(h, x, w_cq, kv, w_co, g_post, g_next, l, bm=256):
    m, d = x.shape
    row = pl.BlockSpec((bm, d), lambda i: (i, 0))
    return pl.pallas_call(
        functools.partial(_xattn_kernel, br=bm),
        grid=(m // bm,),
        in_specs=[row, row,
                  _layer_spec(l, (d, X_WIDTH)),
                  _const_spec((MEM_LEN, 2 * X_WIDTH)),
                  _layer_spec(l, (X_WIDTH, d)),
                  _layer_spec(l, (1, d)), _layer_spec(l, (1, d))],
        out_specs=[row, row],
        out_shape=[jax.ShapeDtypeStruct((m, d), F32), jax.ShapeDtypeStruct((m, d), BF16)],
        compiler_params=_params("arbitrary"),
        name="mem_xattn",
    )(h, x, w_cq, kv, w_co, g_post, g_next)


def _ffn_up_kernel(h_ref, wg_ref, wu_ref, cw_ref, cb_ref, o_ref, gbuf_ref, carry_ref, *, bm, br):
    i = pl.program_id(0)
    j = pl.program_id(1)

    @pl.when(i == 0)
    def _():
        gbuf_ref[0:CONV_HALO, :] = jnp.zeros((CONV_HALO, gbuf_ref.shape[1]), F32)

    @pl.when(i > 0)
    def _():
        gbuf_ref[0:CONV_HALO, :] = carry_ref[j]

    for r0 in range(0, bm, br):
        h = h_ref[r0:r0 + br, :]
        g = _dot(h, wg_ref[...])
        u = _dot(h, wu_ref[...])
        b0 = CONV_HALO + r0
        gbuf_ref[b0:b0 + br, :] = g
        y = cw_ref[2:3, :] * g + cb_ref[...]
        for tap in range(CONV_W - 1):
            lag = CONV_W - 1 - tap
            y = y + cw_ref[tap:tap + 1, :] * gbuf_ref[b0 - lag:b0 - lag + br, :]
        act = y * (1.0 / (1.0 + jnp.exp(-y))) * u
        o_ref[r0:r0 + br, :] = act.astype(o_ref.dtype)

    carry_ref[j] = gbuf_ref[bm:bm + CONV_HALO, :]


def _ffn_up(h, w_gate, w_up, conv_w, conv_b, l, bm, bn):
    m, d = h.shape
    f = w_gate.shape[-1]
    nj = f // bn
    return pl.pallas_call(
        functools.partial(_ffn_up_kernel, bm=bm, br=256),
        grid=(m // bm, nj),
        in_specs=[pl.BlockSpec((bm, d), lambda i, j: (i, 0)),
                  pl.BlockSpec((None, d, bn), lambda i, j: (l, 0, j)),
                  pl.BlockSpec((None, d, bn), lambda i, j: (l, 0, j)),
                  pl.BlockSpec((None, CONV_W, bn), lambda i, j: (l, 0, j)),
                  pl.BlockSpec((None, 1, bn), lambda i, j: (l, 0, j))],
        out_specs=pl.BlockSpec((bm, bn), lambda i, j: (i, j)),
        out_shape=jax.ShapeDtypeStruct((m, f), BF16),
        scratch_shapes=[pltpu.VMEM((CONV_HALO + bm, bn), F32), pltpu.VMEM((nj, CONV_HALO, bn), F32)],
        compiler_params=_params("arbitrary", "arbitrary"),
        name="ffn_up",
    )(h, w_gate, w_up, conv_w, conv_b)


def _prep_weights(w_in, w_uq, w_ukv, w_pool, w_out, w_cq, w_ck, w_cv, w_co, w_gate, w_up, conv_w, conv_b, w_down):
    c1 = Q_LORA + KV_LORA
    c2 = c1 + QK_ROPE
    zr = jnp.zeros((DEPTH, D_MODEL, HALF_ROPE), BF16)
    w_in = w_in.astype(BF16)
    w_a = jnp.concatenate([w_in[..., :c1], w_in[..., c1:c1 + HALF_ROPE], zr,
                           w_in[..., c1 + HALF_ROPE:c2], zr], axis=-1)

    qscale = (QK_NOPE + QK_ROPE) ** -0.5 * math.log2(math.e)
    wq = (w_uq * qscale).astype(BF16).reshape(DEPTH, Q_LORA, MLA_HEADS, QK_NOPE + QK_ROPE)
    zq = jnp.zeros((DEPTH, Q_LORA, MLA_HEADS, HALF_ROPE), BF16)
    wq = jnp.concatenate([wq[..., :QK_NOPE], wq[..., QK_NOPE:QK_NOPE + HALF_ROPE], zq,
                          wq[..., QK_NOPE + HALF_ROPE:], zq], axis=-1)
    wkv = w_ukv.astype(BF16).reshape(DEPTH, KV_LORA, MLA_HEADS, QK_NOPE + V_DIM)
    pad_f = D_FF_PAD - D_FF
    return dict(
        w_a=w_a,
        w_u=w_in[..., c2:],
        w_uqt=jnp.swapaxes(wq.reshape(DEPTH, Q_LORA, MLA_HEADS * QK_PAD), 1, 2),
        w_uk=wkv[..., :QK_NOPE].reshape(DEPTH, KV_LORA, MLA_HEADS * QK_NOPE),
        w_uvt=jnp.swapaxes(wkv[..., QK_NOPE:].reshape(DEPTH, KV_LORA, MLA_HEADS * V_DIM), 1, 2),
        w_pool=w_pool.astype(BF16),
        w_out=w_out.astype(BF16),
        w_cq=(w_cq * X_HEAD_DIM ** -0.5).astype(BF16),
        w_ckv=jnp.concatenate([w_ck.astype(BF16), w_cv.astype(BF16)], axis=-1),
        w_co=w_co.astype(BF16),
        w_gate=jnp.pad(w_gate.astype(BF16), ((0, 0), (0, 0), (0, pad_f))),
        w_up=jnp.pad(w_up.astype(BF16), ((0, 0), (0, 0), (0, pad_f))),
        conv_w=jnp.pad(conv_w, ((0, 0), (0, 0), (0, pad_f))),
        conv_b=jnp.pad(conv_b, ((0, 0), (0, pad_f))).reshape(DEPTH, 1, D_FF_PAD),
        w_down=jnp.pad(w_down.astype(BF16), ((0, 0), (0, pad_f), (0, 0))),
    )


def _gain(g):
    return g.reshape(DEPTH, 1, -1)


def kernel(x, mem, positions, g_mix_pre, g_mix_post, w_in, g_q, w_uq, g_kv, w_ukv, w_pool, s_pool, w_out,
           g_x_pre, g_x_post, g_mem, w_cq, w_ck, w_cv, w_co, g_ffn_pre, g_ffn_post, w_gate, w_up,
           conv_w, conv_b, w_down):
    x = x.reshape(SEQ, D_MODEL)
    mem2 = mem.reshape(MEM_LEN, D_MODEL)
    p = _prep_weights(w_in, w_uq, w_ukv, w_pool, w_out, w_cq, w_ck, w_cv, w_co,
                      w_gate, w_up, conv_w, conv_b, w_down)
    g_mix_pre, g_mix_post, g_q, g_kv, s_pool = map(_gain, (g_mix_pre, g_mix_post, g_q, g_kv, s_pool))
    g_x_pre, g_x_post, g_mem, g_ffn_pre, g_ffn_post = map(_gain, (g_x_pre, g_x_post, g_mem, g_ffn_pre, g_ffn_post))
    cosk, sink, cost, sint = _rope_tables(positions)
    h = _norm(x, g_mix_pre, 0)
    for l in range(DEPTH):
        z_a = _mm(h, p["w_a"], l, 512, F32, "in_proj")
        pooled = _pool(h, p["w_u"], p["w_pool"], s_pool, l)
        qt = _qproj(z_a, g_q, p["w_uqt"], cost, sint, l)
        k, vt = _kvproj(z_a, g_kv, p["w_uk"], p["w_uvt"], cosk, sink, l, ATT_BLK)
        att = _attention(qt, k, vt, ATT_BLK)
        x, h = _mm_res(att, pooled, p["w_out"], x, g_mix_post, g_x_pre, l, l, 1024, 512, 256, "out_proj")
        kv = _norm_mm(mem2, g_mem, p["w_ckv"], l, 512, "mem_kv")
        x, h = _xattn(h, x, p["w_cq"], kv, p["w_co"], g_x_post, g_ffn_pre, l)
        act = _ffn_up(h, p["w_gate"], p["w_up"], p["conv_w"], p["conv_b"], l, 1024, 512)
        last = l + 1 == DEPTH
        res = _mm_res(act, None, p["w_down"], x, g_ffn_post, None if last else g_mix_pre, l, l + 1,
                      1024, 512, 256, "ffn_down")
        x = res[0]
        h = None if last else res[1]
    return x.reshape(1, SEQ, D_MODEL)
```

```python
import functools
import math

import jax
import jax.numpy as jnp
from jax import lax
from jax.experimental import pallas as pl
from jax.experimental.pallas import tpu as pltpu

F32 = jnp.float32
BF16 = jnp.bfloat16

D_MODEL = 4096
SEQ = 16384
DEPTH = 2
CHUNK = 64
MLA_HEADS = 16
QK_NOPE = 128
QK_ROPE = 64
HALF_ROPE = QK_ROPE // 2
V_DIM = 128
Q_LORA = 1024
KV_LORA = 512
ROPE_THETA = 10000.0
POOL_WINDOWS = (2, 4, 8, 16)
POOL_CH = 2048
POOL_GROUP = 512
POOL_HALO = 16
MEM_LEN = 256
X_HEADS = 4
X_HEAD_DIM = 256
X_WIDTH = X_HEADS * X_HEAD_DIM
D_FF = 11008
D_FF_PAD = 11264
CONV_W = 3
CONV_HALO = 8
EPS = 1e-6

QK_PAD = 256
ROPE_LANES = 128
Z_A = Q_LORA + KV_LORA + ROPE_LANES
ATT_BLK = 512
V_ROWS = V_DIM + 16

V7X_VMEM_LIMIT = 56 * 1024 * 1024


def _params(*sem):
    return pltpu.CompilerParams(dimension_semantics=sem, vmem_limit_bytes=V7X_VMEM_LIMIT)


def _const_spec(shape):
    return pl.BlockSpec(shape, lambda *_: (0,) * len(shape), pipeline_mode=pl.Buffered(1))


def _layer_spec(l, shape):
    return pl.BlockSpec((None,) + tuple(shape), lambda *_: (l,) + (0,) * len(shape),
                        pipeline_mode=pl.Buffered(1))


def _rms(x, g):
    return x * lax.rsqrt(jnp.mean(x * x, axis=-1, keepdims=True) + EPS) * g


def _dot(a, b):
    return jnp.dot(a, b, preferred_element_type=F32)


def _dot_nt(a, b):
    return lax.dot_general(a, b, (((1,), (1,)), ((), ())), preferred_element_type=F32)


def _rope_kernel(pc_ref, pr_ref, invr_ref, sgnr_ref, invc_ref, sgnc_ref,
                 cosk_ref, sink_ref, cost_ref, sint_ref):
    ang = pc_ref[...].astype(F32) * invr_ref[...]
    cosk_ref[...] = jnp.cos(ang)
    sink_ref[...] = jnp.sin(ang) * sgnr_ref[...]
    angt = invc_ref[...] * pr_ref[...].astype(F32)
    cost_ref[...] = jnp.cos(angt)
    sint_ref[...] = jnp.sin(angt) * sgnc_ref[...]


def _rope_tables(positions, bs=2048):
    inv = 1.0 / (ROPE_THETA ** (jnp.arange(0, QK_ROPE, 2, dtype=F32) / QK_ROPE))
    zero = jnp.zeros((HALF_ROPE,), F32)
    one = jnp.ones((HALF_ROPE,), F32)
    inv4 = jnp.concatenate([inv, zero, inv, zero])
    sgn4 = jnp.concatenate([-one, zero, one, zero])
    pc = positions.reshape(SEQ, 1)
    pr = positions.reshape(1, SEQ)
    nat = pl.BlockSpec((bs, ROPE_LANES), lambda i: (i, 0))
    tr = pl.BlockSpec((ROPE_LANES, bs), lambda i: (0, i))
    return pl.pallas_call(
        _rope_kernel,
        grid=(SEQ // bs,),
        in_specs=[pl.BlockSpec((bs, 1), lambda i: (i, 0)),
                  pl.BlockSpec((1, bs), lambda i: (0, i)),
                  _const_spec((1, ROPE_LANES)), _const_spec((1, ROPE_LANES)),
                  _const_spec((ROPE_LANES, 1)), _const_spec((ROPE_LANES, 1))],
        out_specs=[nat, nat, tr, tr],
        out_shape=[jax.ShapeDtypeStruct((SEQ, ROPE_LANES), F32)] * 2
        + [jax.ShapeDtypeStruct((ROPE_LANES, SEQ), F32)] * 2,
        compiler_params=_params("arbitrary"),
        name="rope_tables",
    )(pc, pr, inv4.reshape(1, -1), sgn4.reshape(1, -1), inv4.reshape(-1, 1), sgn4.reshape(-1, 1))


def _norm_kernel(x_ref, g_ref, o_ref):
    o_ref[...] = _rms(x_ref[...], g_ref[...]).astype(o_ref.dtype)


def _norm(x, g, l, bm=512):
    m, d = x.shape
    return pl.pallas_call(
        _norm_kernel,
        grid=(m // bm,),
        in_specs=[pl.BlockSpec((bm, d), lambda i: (i, 0)), _layer_spec(l, (1, d))],
        out_specs=pl.BlockSpec((bm, d), lambda i: (i, 0)),
        out_shape=jax.ShapeDtypeStruct((m, d), BF16),
        compiler_params=_params("arbitrary"),
        name="norm",
    )(x, g)


def _mm_kernel(a_ref, w_ref, o_ref):
    o_ref[...] = _dot(a_ref[...], w_ref[...]).astype(o_ref.dtype)


def _mm(a, w, l, bm, out_dtype, name):
    m, k = a.shape
    n = w.shape[-1]
    return pl.pallas_call(
        _mm_kernel,
        grid=(m // bm,),
        in_specs=[pl.BlockSpec((bm, k), lambda i: (i, 0)), _layer_spec(l, (k, n))],
        out_specs=pl.BlockSpec((bm, n), lambda i: (i, 0)),
        out_shape=jax.ShapeDtypeStruct((m, n), out_dtype),
        compiler_params=_params("arbitrary"),
        name=name,
    )(a, w)


def _norm_mm_kernel(a_ref, g_ref, w_ref, o_ref):
    a = _rms(a_ref[...], g_ref[...]).astype(BF16)
    o_ref[...] = _dot(a, w_ref[...]).astype(o_ref.dtype)


def _norm_mm(a, g, w, l, bn, name):
    m, k = a.shape
    n = w.shape[-1]
    return pl.pallas_call(
        _norm_mm_kernel,
        grid=(n // bn,),
        in_specs=[_const_spec((m, k)), _layer_spec(l, (1, k)),
                  pl.BlockSpec((None, k, bn), lambda j: (l, 0, j))],
        out_specs=pl.BlockSpec((m, bn), lambda j: (0, j)),
        out_shape=jax.ShapeDtypeStruct((m, n), BF16),
        compiler_params=_params("arbitrary"),
        name=name,
    )(a, g, w)


def _pool_kernel(h_ref, wu_ref, wp_ref, sp_ref, o_ref, buf_ref, carry_ref, *, bm):
    i = pl.program_id(0)

    @pl.when(i == 0)
    def _():
        buf_ref[0:POOL_HALO, :] = jnp.zeros((POOL_HALO, POOL_CH), F32)

    @pl.when(i > 0)
    def _():
        buf_ref[0:POOL_HALO, :] = carry_ref[...]

    h = h_ref[...]
    t1 = i * bm + lax.broadcasted_iota(jnp.int32, (bm, 1), 0) + 1
    for g, w in reversed(tuple(enumerate(POOL_WINDOWS))):
        c0, c1 = g * POOL_GROUP, (g + 1) * POOL_GROUP
        ug = _dot(h, wu_ref[:, c0:c1])
        buf_ref[POOL_HALO:POOL_HALO + bm, c0:c1] = ug
        carry_ref[:, c0:c1] = ug[bm - POOL_HALO:bm, :]
        s = ug
        for k in range(1, w):
            s = s + buf_ref[POOL_HALO - k:POOL_HALO - k + bm, c0:c1]
        inv_cnt = 1.0 / jnp.minimum(t1, w).astype(F32)
        p = (s * inv_cnt - ug).astype(BF16)
        y = _dot(p, wp_ref[g]) * sp_ref[:, c0:c1]
        o_ref[:, c0:c1] = y.astype(o_ref.dtype)


def _pool(h, w_u, w_pool, s_pool, l, bm=512):
    m, d = h.shape
    return pl.pallas_call(
        functools.partial(_pool_kernel, bm=bm),
        grid=(m // bm,),
        in_specs=[pl.BlockSpec((bm, d), lambda i: (i, 0)),
                  _layer_spec(l, (d, POOL_CH)),
                  _layer_spec(l, (len(POOL_WINDOWS), POOL_GROUP, POOL_GROUP)),
                  _layer_spec(l, (1, POOL_CH))],
        out_specs=pl.BlockSpec((bm, POOL_CH), lambda i: (i, 0)),
        out_shape=jax.ShapeDtypeStruct((m, POOL_CH), BF16),
        scratch_shapes=[pltpu.VMEM((POOL_HALO + bm, POOL_CH), F32), pltpu.VMEM((POOL_HALO, POOL_CH), F32)],
        compiler_params=_params("arbitrary"),
        name="pool",
    )(h, w_u, w_pool, s_pool)


def _qproj_kernel(cq_ref, g_ref, w_ref, cos_ref, sin_ref, o_ref):
    cqn = _rms(cq_ref[...], g_ref[...]).astype(BF16)
    qt = _dot_nt(w_ref[...], cqn)
    cos = cos_ref[...]
    sin = sin_ref[...]
    half = ROPE_LANES // 2
    for h in range(MLA_HEADS):
        r0 = h * QK_PAD
        o_ref[r0:r0 + QK_NOPE, :] = qt[r0:r0 + QK_NOPE, :].astype(o_ref.dtype)
        blk = qt[r0 + QK_NOPE:r0 + QK_PAD, :]
        swp = jnp.concatenate([blk[half:, :], blk[:half, :]], axis=0)
        o_ref[r0 + QK_NOPE:r0 + QK_PAD, :] = (blk * cos + swp * sin).astype(o_ref.dtype)


def _qproj(z_a, g_q, w_uqt, cost, sint, l, bm=512):
    m = z_a.shape[0]
    rows = MLA_HEADS * QK_PAD
    return pl.pallas_call(
        _qproj_kernel,
        grid=(m // bm,),
        in_specs=[pl.BlockSpec((bm, Q_LORA), lambda i: (i, 0)),
                  _layer_spec(l, (1, Q_LORA)),
                  _layer_spec(l, (rows, Q_LORA)),
                  pl.BlockSpec((ROPE_LANES, bm), lambda i: (0, i)),
                  pl.BlockSpec((ROPE_LANES, bm), lambda i: (0, i))],
        out_specs=pl.BlockSpec((rows, bm), lambda i: (0, i)),
        out_shape=jax.ShapeDtypeStruct((rows, m), BF16),
        compiler_params=_params("arbitrary"),
        name="q_proj",
    )(z_a, g_q, w_uqt, cost, sint)


def _kvproj_kernel(ckv_ref, kr_ref, g_ref, wk_ref, wvt_ref, cos_ref, sin_ref, k_ref, vt_ref):
    ckvn = _rms(ckv_ref[...], g_ref[...]).astype(BF16)
    kn = _dot(ckvn, wk_ref[...])
    vt = _dot_nt(wvt_ref[...], ckvn)
    ones_row = (lax.broadcasted_iota(jnp.int32, (V_ROWS - V_DIM, vt.shape[1]), 0) == 0).astype(vt_ref.dtype)
    for h in range(MLA_HEADS):
        vt_ref[0, h * V_ROWS:h * V_ROWS + V_DIM, :] = vt[h * V_DIM:(h + 1) * V_DIM, :].astype(vt_ref.dtype)
        vt_ref[0, h * V_ROWS + V_DIM:(h + 1) * V_ROWS, :] = ones_row
    kr = kr_ref[...]
    swp = pltpu.roll(kr, ROPE_LANES // 2, 1)
    krope = (kr * cos_ref[...] + swp * sin_ref[...]).astype(k_ref.dtype)
    for h in range(MLA_HEADS):
        k_ref[h, :, 0:QK_NOPE] = kn[:, h * QK_NOPE:(h + 1) * QK_NOPE].astype(k_ref.dtype)
        k_ref[h, :, QK_NOPE:QK_PAD] = krope


def _kvproj(z_a, g_kv, w_uk, w_uvt, cosk, sink, l, bm):
    m = z_a.shape[0]
    return pl.pallas_call(
        _kvproj_kernel,
        grid=(m // bm,),
        in_specs=[pl.BlockSpec((bm, KV_LORA), lambda i: (i, Q_LORA // KV_LORA)),
                  pl.BlockSpec((bm, ROPE_LANES), lambda i: (i, (Q_LORA + KV_LORA) // ROPE_LANES)),
                  _layer_spec(l, (1, KV_LORA)),
                  _layer_spec(l, (KV_LORA, MLA_HEADS * QK_NOPE)),
                  _layer_spec(l, (MLA_HEADS * V_DIM, KV_LORA)),
                  pl.BlockSpec((bm, ROPE_LANES), lambda i: (i, 0)),
                  pl.BlockSpec((bm, ROPE_LANES), lambda i: (i, 0))],
        out_specs=[pl.BlockSpec((MLA_HEADS, bm, QK_PAD), lambda i: (0, i, 0)),
                   pl.BlockSpec((1, MLA_HEADS * V_ROWS, bm), lambda i: (i, 0, 0))],
        out_shape=[jax.ShapeDtypeStruct((MLA_HEADS, m, QK_PAD), BF16),
                   jax.ShapeDtypeStruct((m // bm, MLA_HEADS * V_ROWS, bm), BF16)],
        compiler_params=_params("arbitrary"),
        name="kv_proj",
    )(z_a, z_a, g_kv, w_uk, w_uvt, cosk, sink)


def _attn_kernel(qt_ref, k_ref, vt_ref, o_ref, s0, s1, p0, p1, m_ref, alpha_ref, cmax_ref, acc_ref, *, blk):
    i = pl.program_id(1)
    q = qt_ref[...]
    sbuf = (s0, s1)
    pbuf = (p0, p1)

    def scores(jn, s_w):
        off = pl.multiple_of(jn * blk, blk)
        s = _dot(k_ref[0, pl.ds(off, blk), :], q)
        s_w[...] = s
        return jnp.max(s, axis=0, keepdims=True)

    def probs(s, cmax):
        m = m_ref[...]
        m_new = jnp.maximum(m, cmax)
        m_ref[...] = m_new
        return jnp.exp2(m - m_new), jnp.exp2(s - m_new).astype(BF16)

    def values(alpha, p, jv):
        acc_ref[...] = alpha * acc_ref[...] + _dot(vt_ref[jv], p)

    def stage(j, par):
        s_r, s_w = sbuf[par], sbuf[1 - par]
        p_w, p_r = pbuf[par], pbuf[1 - par]
        alpha_prev, p_prev, cmax, s_cur = alpha_ref[...], p_r[...], cmax_ref[...], s_r[...]
        cmax_ref[...] = scores(j + 1, s_w)
        values(alpha_prev, p_prev, jnp.maximum(j - 1, 0))
        alpha, p = probs(s_cur, cmax)
        alpha_ref[...] = alpha
        p_w[...] = p

    def finish(par):
        s_r, p_r = sbuf[par], pbuf[1 - par]
        values(alpha_ref[...], p_r[...], jnp.maximum(i - 1, 0))
        qc = lax.broadcasted_iota(jnp.int32, (1, blk), 1) // CHUNK
        s = jnp.concatenate(
            [jnp.where(qc >= r, s_r[r * CHUNK:(r + 1) * CHUNK, :], -jnp.inf) for r in range(blk // CHUNK)],
            axis=0)
        alpha, p = probs(s, jnp.max(s, axis=0, keepdims=True))
        values(alpha, p, i)
        o_ref[...] = (acc_ref[0:V_DIM, :] / acc_ref[V_DIM:V_DIM + 1, :]).T.astype(o_ref.dtype)

    p1[...] = jnp.zeros_like(p1)
    m_ref[...] = jnp.full(m_ref.shape, -jnp.inf, F32)
    alpha_ref[...] = jnp.ones_like(alpha_ref)
    acc_ref[...] = jnp.zeros_like(acc_ref)
    cmax_ref[...] = scores(0, s0)

    def pair(t, c):
        stage(2 * t, 0)
        stage(2 * t + 1, 1)
        return c

    lax.fori_loop(0, i // 2, pair, 0)

    @pl.when(i % 2 == 1)
    def _():
        stage(i - 1, 0)
        finish(1)

    @pl.when(i % 2 == 0)
    def _():
        finish(0)


def _attention(qt, k, vt, blk):
    m = k.shape[1]
    nb = m // blk
    row = pltpu.VMEM((1, blk), F32)
    return pl.pallas_call(
        functools.partial(_attn_kernel, blk=blk),
        grid=(MLA_HEADS, nb),
        in_specs=[pl.BlockSpec((QK_PAD, blk), lambda h, i: (h, i)),
                  pl.BlockSpec((1, m, QK_PAD), lambda h, i: (h, 0, 0)),
                  pl.BlockSpec((nb, V_ROWS, blk), lambda h, i: (0, h, 0))],
        out_specs=pl.BlockSpec((blk, V_DIM), lambda h, i: (i, h)),
        out_shape=jax.ShapeDtypeStruct((m, MLA_HEADS * V_DIM), BF16),
        scratch_shapes=[pltpu.VMEM((blk, blk), F32)] * 2 + [pltpu.VMEM((blk, blk), BF16)] * 2
        + [row, row, row, pltpu.VMEM((V_ROWS, blk), F32)],
        compiler_params=_params("arbitrary", "arbitrary"),
        name="mla_attention",
    )(qt, k, vt)


def _residual_epilogue(acc, x_ref, gpost_ref, gnext_ref, xo_ref, ho_ref):
    xn = x_ref[...] + _rms(acc, gpost_ref[...])
    xo_ref[...] = xn
    if ho_ref is not None:
        ho_ref[...] = _rms(xn, gnext_ref[...]).astype(ho_ref.dtype)


def _mm_res_kernel(a1_ref, a2_ref, w_ref, x_ref, gpost_ref, gnext_ref, xo_ref, *rest, n1, nk, be, bc):
    ho_ref = rest[0] if len(rest) == 2 else None
    acc_ref = rest[-1]
    k = pl.program_id(1)
    n = acc_ref.shape[1]

    def accumulate(a_ref, first):
        a = a_ref[...]
        for c0 in range(0, n, bc):
            d = _dot(a, w_ref[:, c0:c0 + bc])
            if first:
                acc_ref[:, c0:c0 + bc] = d
            else:
                acc_ref[:, c0:c0 + bc] += d

    @pl.when(k == 0)
    def _():
        accumulate(a1_ref, True)

    if n1 > 1:
        @pl.when(jnp.logical_and(k > 0, k < n1))
        def _():
            accumulate(a1_ref, False)

    if n1 < nk:
        @pl.when(jnp.logical_and(k >= n1, k < nk))
        def _():
            accumulate(a2_ref, False)

    @pl.when(k >= nk)
    def _():
        r0 = pl.multiple_of((k - nk) * be, be)
        _residual_epilogue(acc_ref[pl.ds(r0, be), :], x_ref, gpost_ref, gnext_ref, xo_ref, ho_ref)


def _mm_res(a1, a2, w, x, g_post, g_next, l, l_next, bm, bk, be, name):
    m, n = x.shape
    n1 = a1.shape[1] // bk
    nk = w.shape[1] // bk
    ne = bm // be
    if a2 is None:
        a2 = a1
    n2 = max(nk - n1, 1)
    has_next = g_next is not None
    if not has_next:
        g_next, l_next = g_post, l
    rows = lambda i, k: (i * ne + jnp.maximum(k - nk, 0), 0)
    row = pl.BlockSpec((be, n), rows)
    out_specs = [row] + ([row] if has_next else [])
    out_shape = [jax.ShapeDtypeStruct((m, n), F32)] + ([jax.ShapeDtypeStruct((m, n), BF16)] if has_next else [])
    return pl.pallas_call(
        functools.partial(_mm_res_kernel, n1=n1, nk=nk, be=be, bc=512),
        grid=(m // bm, nk + ne),
        in_specs=[pl.BlockSpec((bm, bk), lambda i, k: (i, jnp.minimum(k, n1 - 1))),
                  pl.BlockSpec((bm, bk), lambda i, k: (i, jnp.clip(k - n1, 0, n2 - 1))),
                  pl.BlockSpec((None, bk, n), lambda i, k: (l, jnp.minimum(k, nk - 1), 0)),
                  row, _layer_spec(l, (1, n)), _layer_spec(l_next, (1, n))],
        out_specs=out_specs,
        out_shape=out_shape,
        scratch_shapes=[pltpu.VMEM((bm, n), F32)],
        compiler_params=_params("arbitrary", "arbitrary"),
        name=name,
    )(a1, a2, w, x, g_post, g_next)


def _xattn_kernel(h_ref, x_ref, wq_ref, kv_ref, wo_ref, gpost_ref, gnext_ref, xo_ref, ho_ref, *, br):
    for r0 in range(0, h_ref.shape[0], br):
        rows = slice(r0, r0 + br)
        q = _dot(h_ref[rows, :], wq_ref[...]).astype(BF16)
        outs = []
        for hh in range(X_HEADS):
            c0, c1 = hh * X_HEAD_DIM, (hh + 1) * X_HEAD_DIM
            s = _dot_nt(q[:, c0:c1], kv_ref[:, c0:c1])
            p = jnp.exp(s - jnp.max(s, axis=-1, keepdims=True))
            l = jnp.sum(p, axis=-1, keepdims=True)
            o = _dot(p.astype(BF16), kv_ref[:, X_WIDTH + c0:X_WIDTH + c1]) / l
            outs.append(o.astype(BF16))
        c = _dot(jnp.concatenate(outs, axis=-1), wo_ref[...])
        xn = x_ref[rows, :] + _rms(c, gpost_ref[...])
        xo_ref[rows, :] = xn
        ho_ref[rows, :] = _rms(xn, gnext_ref[...]).astype(ho_ref.dtype)


def _xattn(h, x, w_cq, kv, w_co, g_post, g_next, l, bm=256):
    m, d = x.shape
    row = pl.BlockSpec((bm, d), lambda i: (i, 0))
    return pl.pallas_call(
        functools.partial(_xattn_kernel, br=bm),
        grid=(m // bm,),
        in_specs=[row, row,
                  _layer_spec(l, (d, X_WIDTH)),
                  _const_spec((MEM_LEN, 2 * X_WIDTH)),
                  _layer_spec(l, (X_WIDTH, d)),
                  _layer_spec(l, (1, d)), _layer_spec(l, (1, d))],
        out_specs=[row, row],
        out_shape=[jax.ShapeDtypeStruct((m, d), F32), jax.ShapeDtypeStruct((m, d), BF16)],
        compiler_params=_params("arbitrary"),
        name="mem_xattn",
    )(h, x, w_cq, kv, w_co, g_post, g_next)


def _ffn_up_kernel(h_ref, wg_ref, wu_ref, cw_ref, cb_ref, o_ref, gbuf_ref, carry_ref, *, bm, br):
    i = pl.program_id(0)
    j = pl.program_id(1)

    @pl.when(i == 0)
    def _():
        gbuf_ref[0:CONV_HALO, :] = jnp.zeros((CONV_HALO, gbuf_ref.shape[1]), F32)

    @pl.when(i > 0)
    def _():
        gbuf_ref[0:CONV_HALO, :] = carry_ref[j]

    for r0 in range(0, bm, br):
        h = h_ref[r0:r0 + br, :]
        g = _dot(h, wg_ref[...])
        u = _dot(h, wu_ref[...])
        b0 = CONV_HALO + r0
        gbuf_ref[b0:b0 + br, :] = g
        y = cw_ref[2:3, :] * g + cb_ref[...]
        for tap in range(CONV_W - 1):
            lag = CONV_W - 1 - tap
            y = y + cw_ref[tap:tap + 1, :] * gbuf_ref[b0 - lag:b0 - lag + br, :]
        act = y * (1.0 / (1.0 + jnp.exp(-y))) * u
        o_ref[r0:r0 + br, :] = act.astype(o_ref.dtype)

    carry_ref[j] = gbuf_ref[bm:bm + CONV_HALO, :]


def _ffn_up(h, w_gate, w_up, conv_w, conv_b, l, bm, bn):
    m, d = h.shape
    f = w_gate.shape[-1]
    nj = f // bn
    return pl.pallas_call(
        functools.partial(_ffn_up_kernel, bm=bm, br=256),
        grid=(m // bm, nj),
        in_specs=[pl.BlockSpec((bm, d), lambda i, j: (i, 0)),
                  pl.BlockSpec((None, d, bn), lambda i, j: (l, 0, j)),
                  pl.BlockSpec((None, d, bn), lambda i, j: (l, 0, j)),
                  pl.BlockSpec((None, CONV_W, bn), lambda i, j: (l, 0, j)),
                  pl.BlockSpec((None, 1, bn), lambda i, j: (l, 0, j))],
        out_specs=pl.BlockSpec((bm, bn), lambda i, j: (i, j)),
        out_shape=jax.ShapeDtypeStruct((m, f), BF16),
        scratch_shapes=[pltpu.VMEM((CONV_HALO + bm, bn), F32), pltpu.VMEM((nj, CONV_HALO, bn), F32)],
        compiler_params=_params("arbitrary", "arbitrary"),
        name="ffn_up",
    )(h, w_gate, w_up, conv_w, conv_b)


def _cast_pad_cols_kernel(x_ref, o_ref):
    c = x_ref.shape[-1]
    o_ref[:, 0:c] = x_ref[...].astype(o_ref.dtype)
    o_ref[:, c:] = jnp.zeros((o_ref.shape[0], o_ref.shape[1] - c), o_ref.dtype)


def _cast_pad_cols(w, cols, br=128):
    _, r, c = w.shape
    return pl.pallas_call(
        _cast_pad_cols_kernel,
        grid=(DEPTH, r // br),
        in_specs=[pl.BlockSpec((None, br, c), lambda l, i: (l, i, 0))],
        out_specs=pl.BlockSpec((None, br, cols), lambda l, i: (l, i, 0)),
        out_shape=jax.ShapeDtypeStruct((DEPTH, r, cols), BF16),
        compiler_params=_params("arbitrary", "arbitrary"),
        name="cast_pad_cols",
    )(w)


def _cast_pad_rows_kernel(x_ref, o_ref, *, nvalid):
    i = pl.program_id(1)

    @pl.when(i < nvalid)
    def _():
        o_ref[...] = x_ref[...].astype(o_ref.dtype)

    @pl.when(i >= nvalid)
    def _():
        o_ref[...] = jnp.zeros_like(o_ref)


def _cast_pad_rows(w, rows, br=256):
    _, r, c = w.shape
    nvalid = r // br
    return pl.pallas_call(
        functools.partial(_cast_pad_rows_kernel, nvalid=nvalid),
        grid=(DEPTH, rows // br),
        in_specs=[pl.BlockSpec((None, br, c), lambda l, i: (l, jnp.minimum(i, nvalid - 1), 0))],
        out_specs=pl.BlockSpec((None, br, c), lambda l, i: (l, i, 0)),
        out_shape=jax.ShapeDtypeStruct((DEPTH, rows, c), BF16),
        compiler_params=_params("arbitrary", "arbitrary"),
        name="cast_pad_rows",
    )(w)


def _prep_weights(w_in, w_uq, w_ukv, w_pool, w_out, w_cq, w_ck, w_cv, w_co, w_gate, w_up, conv_w, conv_b, w_down):
    c1 = Q_LORA + KV_LORA
    c2 = c1 + QK_ROPE
    zr = jnp.zeros((DEPTH, D_MODEL, HALF_ROPE), BF16)
    w_in = w_in.astype(BF16)
    w_a = jnp.concatenate([w_in[..., :c1], w_in[..., c1:c1 + HALF_ROPE], zr,
                           w_in[..., c1 + HALF_ROPE:c2], zr], axis=-1)

    qscale = (QK_NOPE + QK_ROPE) ** -0.5 * math.log2(math.e)
    wq = (w_uq * qscale).astype(BF16).reshape(DEPTH, Q_LORA, MLA_HEADS, QK_NOPE + QK_ROPE)
    zq = jnp.zeros((DEPTH, Q_LORA, MLA_HEADS, HALF_ROPE), BF16)
    wq = jnp.concatenate([wq[..., :QK_NOPE], wq[..., QK_NOPE:QK_NOPE + HALF_ROPE], zq,
                          wq[..., QK_NOPE + HALF_ROPE:], zq], axis=-1)
    wkv = w_ukv.astype(BF16).reshape(DEPTH, KV_LORA, MLA_HEADS, QK_NOPE + V_DIM)
    pad_f = D_FF_PAD - D_FF
    return dict(
        w_a=w_a,
        w_u=w_in[..., c2:],
        w_uqt=jnp.swapaxes(wq.reshape(DEPTH, Q_LORA, MLA_HEADS * QK_PAD), 1, 2),
        w_uk=wkv[..., :QK_NOPE].reshape(DEPTH, KV_LORA, MLA_HEADS * QK_NOPE),
        w_uvt=jnp.swapaxes(wkv[..., QK_NOPE:].reshape(DEPTH, KV_LORA, MLA_HEADS * V_DIM), 1, 2),
        w_pool=w_pool.astype(BF16),
        w_out=w_out.astype(BF16),
        w_cq=(w_cq * X_HEAD_DIM ** -0.5).astype(BF16),
        w_ckv=jnp.concatenate([w_ck.astype(BF16), w_cv.astype(BF16)], axis=-1),
        w_co=w_co.astype(BF16),
        w_gate=_cast_pad_cols(w_gate, D_FF_PAD),
        w_up=_cast_pad_cols(w_up, D_FF_PAD),
        conv_w=jnp.pad(conv_w, ((0, 0), (0, 0), (0, pad_f))),
        conv_b=jnp.pad(conv_b, ((0, 0), (0, pad_f))).reshape(DEPTH, 1, D_FF_PAD),
        w_down=_cast_pad_rows(w_down, D_FF_PAD),
    )


def _gain(g):
    return g.reshape(DEPTH, 1, -1)


def kernel(x, mem, positions, g_mix_pre, g_mix_post, w_in, g_q, w_uq, g_kv, w_ukv, w_pool, s_pool, w_out,
           g_x_pre, g_x_post, g_mem, w_cq, w_ck, w_cv, w_co, g_ffn_pre, g_ffn_post, w_gate, w_up,
           conv_w, conv_b, w_down):
    x = x.reshape(SEQ, D_MODEL)
    mem2 = mem.reshape(MEM_LEN, D_MODEL)
    p = _prep_weights(w_in, w_uq, w_ukv, w_pool, w_out, w_cq, w_ck, w_cv, w_co,
                      w_gate, w_up, conv_w, conv_b, w_down)
    g_mix_pre, g_mix_post, g_q, g_kv, s_pool = map(_gain, (g_mix_pre, g_mix_post, g_q, g_kv, s_pool))
    g_x_pre, g_x_post, g_mem, g_ffn_pre, g_ffn_post = map(_gain, (g_x_pre, g_x_post, g_mem, g_ffn_pre, g_ffn_post))
    cosk, sink, cost, sint = _rope_tables(positions)
    h = _norm(x, g_mix_pre, 0)
    for l in range(DEPTH):
        z_a = _mm(h, p["w_a"], l, 512, F32, "in_proj")
        pooled = _pool(h, p["w_u"], p["w_pool"], s_pool, l)
        qt = _qproj(z_a, g_q, p["w_uqt"], cost, sint, l)
        k, vt = _kvproj(z_a, g_kv, p["w_uk"], p["w_uvt"], cosk, sink, l, ATT_BLK)
        att = _attention(qt, k, vt, ATT_BLK)
        x, h = _mm_res(att, pooled, p["w_out"], x, g_mix_post, g_x_pre, l, l, 1024, 512, 256, "out_proj")
        kv = _norm_mm(mem2, g_mem, p["w_ckv"], l, 512, "mem_kv")
        x, h = _xattn(h, x, p["w_cq"], kv, p["w_co"], g_x_post, g_ffn_pre, l)
        act = _ffn_up(h, p["w_gate"], p["w_up"], p["conv_w"], p["conv_b"], l, 1024, 512)
        last = l + 1 == DEPTH
        res = _mm_res(act, None, p["w_down"], x, g_ffn_post, None if last else g_mix_pre, l, l + 1,
                      1024, 512, 256, "ffn_down")
        x = res[0]
        h = None if last else res[1]
    return x.reshape(1, SEQ, D_MODEL)
```

```python
import functools
import math

import jax
import jax.numpy as jnp
from jax import lax
from jax.experimental import pallas as pl
from jax.experimental.pallas import tpu as pltpu

F32 = jnp.float32
BF16 = jnp.bfloat16

D_MODEL = 4096
SEQ = 16384
DEPTH = 2
CHUNK = 64
MLA_HEADS = 16
QK_NOPE = 128
QK_ROPE = 64
HALF_ROPE = QK_ROPE // 2
V_DIM = 128
Q_LORA = 1024
KV_LORA = 512
ROPE_THETA = 10000.0
POOL_WINDOWS = (2, 4, 8, 16)
POOL_CH = 2048
POOL_GROUP = 512
POOL_HALO = 16
MEM_LEN = 256
X_HEADS = 4
X_HEAD_DIM = 256
X_WIDTH = X_HEADS * X_HEAD_DIM
D_FF = 11008
D_FF_PAD = 11264
CONV_W = 3
CONV_HALO = 8
EPS = 1e-6

QK_PAD = 256
ROPE_LANES = 128
Z_A = Q_LORA + KV_LORA + ROPE_LANES
ATT_BLK = 512
ATT_HEADS = 2
V_ROWS = V_DIM + 16

V7X_VMEM_LIMIT = 56 * 1024 * 1024


def _params(*sem):
    return pltpu.CompilerParams(dimension_semantics=sem, vmem_limit_bytes=V7X_VMEM_LIMIT)


def _const_spec(shape):
    return pl.BlockSpec(shape, lambda *_: (0,) * len(shape), pipeline_mode=pl.Buffered(1))


def _layer_spec(l, shape):
    return pl.BlockSpec((None,) + tuple(shape), lambda *_: (l,) + (0,) * len(shape),
                        pipeline_mode=pl.Buffered(1))


def _rms(x, g):
    return x * lax.rsqrt(jnp.mean(x * x, axis=-1, keepdims=True) + EPS) * g


def _dot(a, b):
    return jnp.dot(a, b, preferred_element_type=F32)


def _dot_nt(a, b):
    return lax.dot_general(a, b, (((1,), (1,)), ((), ())), preferred_element_type=F32)


def _rope_kernel(pc_ref, pr_ref, invr_ref, sgnr_ref, invc_ref, sgnc_ref,
                 cosk_ref, sink_ref, cost_ref, sint_ref):
    ang = pc_ref[...].astype(F32) * invr_ref[...]
    cosk_ref[...] = jnp.cos(ang)
    sink_ref[...] = jnp.sin(ang) * sgnr_ref[...]
    angt = invc_ref[...] * pr_ref[...].astype(F32)
    cost_ref[...] = jnp.cos(angt)
    sint_ref[...] = jnp.sin(angt) * sgnc_ref[...]


def _rope_tables(positions, bs=2048):
    inv = 1.0 / (ROPE_THETA ** (jnp.arange(0, QK_ROPE, 2, dtype=F32) / QK_ROPE))
    zero = jnp.zeros((HALF_ROPE,), F32)
    one = jnp.ones((HALF_ROPE,), F32)
    inv4 = jnp.concatenate([inv, zero, inv, zero])
    sgn4 = jnp.concatenate([-one, zero, one, zero])
    pc = positions.reshape(SEQ, 1)
    pr = positions.reshape(1, SEQ)
    nat = pl.BlockSpec((bs, ROPE_LANES), lambda i: (i, 0))
    tr = pl.BlockSpec((ROPE_LANES, bs), lambda i: (0, i))
    return pl.pallas_call(
        _rope_kernel,
        grid=(SEQ // bs,),
        in_specs=[pl.BlockSpec((bs, 1), lambda i: (i, 0)),
                  pl.BlockSpec((1, bs), lambda i: (0, i)),
                  _const_spec((1, ROPE_LANES)), _const_spec((1, ROPE_LANES)),
                  _const_spec((ROPE_LANES, 1)), _const_spec((ROPE_LANES, 1))],
        out_specs=[nat, nat, tr, tr],
        out_shape=[jax.ShapeDtypeStruct((SEQ, ROPE_LANES), F32)] * 2
        + [jax.ShapeDtypeStruct((ROPE_LANES, SEQ), F32)] * 2,
        compiler_params=_params("arbitrary"),
        name="rope_tables",
    )(pc, pr, inv4.reshape(1, -1), sgn4.reshape(1, -1), inv4.reshape(-1, 1), sgn4.reshape(-1, 1))


def _norm_kernel(x_ref, g_ref, o_ref):
    o_ref[...] = _rms(x_ref[...], g_ref[...]).astype(o_ref.dtype)


def _norm(x, g, l, bm=512):
    m, d = x.shape
    return pl.pallas_call(
        _norm_kernel,
        grid=(m // bm,),
        in_specs=[pl.BlockSpec((bm, d), lambda i: (i, 0)), _layer_spec(l, (1, d))],
        out_specs=pl.BlockSpec((bm, d), lambda i: (i, 0)),
        out_shape=jax.ShapeDtypeStruct((m, d), BF16),
        compiler_params=_params("arbitrary"),
        name="norm",
    )(x, g)


def _mm_kernel(a_ref, w_ref, o_ref):
    o_ref[...] = _dot(a_ref[...], w_ref[...]).astype(o_ref.dtype)


def _mm(a, w, l, bm, out_dtype, name):
    m, k = a.shape
    n = w.shape[-1]
    return pl.pallas_call(
        _mm_kernel,
        grid=(m // bm,),
        in_specs=[pl.BlockSpec((bm, k), lambda i: (i, 0)), _layer_spec(l, (k, n))],
        out_specs=pl.BlockSpec((bm, n), lambda i: (i, 0)),
        out_shape=jax.ShapeDtypeStruct((m, n), out_dtype),
        compiler_params=_params("arbitrary"),
        name=name,
    )(a, w)


def _norm_mm_kernel(a_ref, g_ref, w_ref, o_ref):
    a = _rms(a_ref[...], g_ref[...]).astype(BF16)
    o_ref[...] = _dot(a, w_ref[...]).astype(o_ref.dtype)


def _norm_mm(a, g, w, l, bn, name):
    m, k = a.shape
    n = w.shape[-1]
    return pl.pallas_call(
        _norm_mm_kernel,
        grid=(n // bn,),
        in_specs=[_const_spec((m, k)), _layer_spec(l, (1, k)),
                  pl.BlockSpec((None, k, bn), lambda j: (l, 0, j))],
        out_specs=pl.BlockSpec((m, bn), lambda j: (0, j)),
        out_shape=jax.ShapeDtypeStruct((m, n), BF16),
        compiler_params=_params("arbitrary"),
        name=name,
    )(a, g, w)


def _pool_kernel(h_ref, wu_ref, wp_ref, sp_ref, o_ref, buf_ref, carry_ref, *, bm):
    i = pl.program_id(0)

    @pl.when(i == 0)
    def _():
        buf_ref[0:POOL_HALO, :] = jnp.zeros((POOL_HALO, POOL_CH), F32)

    @pl.when(i > 0)
    def _():
        buf_ref[0:POOL_HALO, :] = carry_ref[...]

    h = h_ref[...]
    t1 = i * bm + lax.broadcasted_iota(jnp.int32, (bm, 1), 0) + 1
    for g, w in reversed(tuple(enumerate(POOL_WINDOWS))):
        c0, c1 = g * POOL_GROUP, (g + 1) * POOL_GROUP
        ug = _dot(h, wu_ref[:, c0:c1])
        buf_ref[POOL_HALO:POOL_HALO + bm, c0:c1] = ug
        carry_ref[:, c0:c1] = ug[bm - POOL_HALO:bm, :]
        s = ug
        for k in range(1, w):
            s = s + buf_ref[POOL_HALO - k:POOL_HALO - k + bm, c0:c1]
        inv_cnt = 1.0 / jnp.minimum(t1, w).astype(F32)
        p = (s * inv_cnt - ug).astype(BF16)
        y = _dot(p, wp_ref[g]) * sp_ref[:, c0:c1]
        o_ref[:, c0:c1] = y.astype(o_ref.dtype)


def _pool(h, w_u, w_pool, s_pool, l, bm=512):
    m, d = h.shape
    return pl.pallas_call(
        functools.partial(_pool_kernel, bm=bm),
        grid=(m // bm,),
        in_specs=[pl.BlockSpec((bm, d), lambda i: (i, 0)),
                  _layer_spec(l, (d, POOL_CH)),
                  _layer_spec(l, (len(POOL_WINDOWS), POOL_GROUP, POOL_GROUP)),
                  _layer_spec(l, (1, POOL_CH))],
        out_specs=pl.BlockSpec((bm, POOL_CH), lambda i: (i, 0)),
        out_shape=jax.ShapeDtypeStruct((m, POOL_CH), BF16),
        scratch_shapes=[pltpu.VMEM((POOL_HALO + bm, POOL_CH), F32), pltpu.VMEM((POOL_HALO, POOL_CH), F32)],
        compiler_params=_params("arbitrary"),
        name="pool",
    )(h, w_u, w_pool, s_pool)


def _qproj_kernel(cq_ref, g_ref, w_ref, cos_ref, sin_ref, o_ref):
    cqn = _rms(cq_ref[...], g_ref[...]).astype(BF16)
    qt = _dot_nt(w_ref[...], cqn)
    cos = cos_ref[...]
    sin = sin_ref[...]
    half = ROPE_LANES // 2
    for h in range(MLA_HEADS):
        r0 = h * QK_PAD
        o_ref[r0:r0 + QK_NOPE, :] = qt[r0:r0 + QK_NOPE, :].astype(o_ref.dtype)
        blk = qt[r0 + QK_NOPE:r0 + QK_PAD, :]
        swp = jnp.concatenate([blk[half:, :], blk[:half, :]], axis=0)
        o_ref[r0 + QK_NOPE:r0 + QK_PAD, :] = (blk * cos + swp * sin).astype(o_ref.dtype)


def _qproj(z_a, g_q, w_uqt, cost, sint, l, bm=512):
    m = z_a.shape[0]
    rows = MLA_HEADS * QK_PAD
    return pl.pallas_call(
        _qproj_kernel,
        grid=(m // bm,),
        in_specs=[pl.BlockSpec((bm, Q_LORA), lambda i: (i, 0)),
                  _layer_spec(l, (1, Q_LORA)),
                  _layer_spec(l, (rows, Q_LORA)),
                  pl.BlockSpec((ROPE_LANES, bm), lambda i: (0, i)),
                  pl.BlockSpec((ROPE_LANES, bm), lambda i: (0, i))],
        out_specs=pl.BlockSpec((rows, bm), lambda i: (0, i)),
        out_shape=jax.ShapeDtypeStruct((rows, m), BF16),
        compiler_params=_params("arbitrary"),
        name="q_proj",
    )(z_a, g_q, w_uqt, cost, sint)


def _kvproj_kernel(ckv_ref, kr_ref, g_ref, wk_ref, wvt_ref, cos_ref, sin_ref, k_ref, vt_ref):
    ckvn = _rms(ckv_ref[...], g_ref[...]).astype(BF16)
    kn = _dot(ckvn, wk_ref[...])
    vt = _dot_nt(wvt_ref[...], ckvn)
    ones_row = (lax.broadcasted_iota(jnp.int32, (V_ROWS - V_DIM, vt.shape[1]), 0) == 0).astype(vt_ref.dtype)
    for h in range(MLA_HEADS):
        vt_ref[0, h * V_ROWS:h * V_ROWS + V_DIM, :] = vt[h * V_DIM:(h + 1) * V_DIM, :].astype(vt_ref.dtype)
        vt_ref[0, h * V_ROWS + V_DIM:(h + 1) * V_ROWS, :] = ones_row
    kr = kr_ref[...]
    swp = pltpu.roll(kr, ROPE_LANES // 2, 1)
    krope = (kr * cos_ref[...] + swp * sin_ref[...]).astype(k_ref.dtype)
    for h in range(MLA_HEADS):
        k_ref[h, :, 0:QK_NOPE] = kn[:, h * QK_NOPE:(h + 1) * QK_NOPE].astype(k_ref.dtype)
        k_ref[h, :, QK_NOPE:QK_PAD] = krope


def _kvproj(z_a, g_kv, w_uk, w_uvt, cosk, sink, l, bm):
    m = z_a.shape[0]
    return pl.pallas_call(
        _kvproj_kernel,
        grid=(m // bm,),
        in_specs=[pl.BlockSpec((bm, KV_LORA), lambda i: (i, Q_LORA // KV_LORA)),
                  pl.BlockSpec((bm, ROPE_LANES), lambda i: (i, (Q_LORA + KV_LORA) // ROPE_LANES)),
                  _layer_spec(l, (1, KV_LORA)),
                  _layer_spec(l, (KV_LORA, MLA_HEADS * QK_NOPE)),
                  _layer_spec(l, (MLA_HEADS * V_DIM, KV_LORA)),
                  pl.BlockSpec((bm, ROPE_LANES), lambda i: (i, 0)),
                  pl.BlockSpec((bm, ROPE_LANES), lambda i: (i, 0))],
        out_specs=[pl.BlockSpec((MLA_HEADS, bm, QK_PAD), lambda i: (0, i, 0)),
                   pl.BlockSpec((1, MLA_HEADS * V_ROWS, bm), lambda i: (i, 0, 0))],
        out_shape=[jax.ShapeDtypeStruct((MLA_HEADS, m, QK_PAD), BF16),
                   jax.ShapeDtypeStruct((m // bm, MLA_HEADS * V_ROWS, bm), BF16)],
        compiler_params=_params("arbitrary"),
        name="kv_proj",
    )(z_a, z_a, g_kv, w_uk, w_uvt, cosk, sink)


def _attn_head(h, i, blk, qt_ref, k_ref, vt_ref, o_ref, s0, s1, p0, p1, m_ref, alpha_ref, cmax_ref, acc_ref):
    q = qt_ref[h * QK_PAD:(h + 1) * QK_PAD, :]
    sbuf = (s0, s1)
    pbuf = (p0, p1)

    def scores(jn, s_w):
        off = pl.multiple_of(jn * blk, blk)
        s = _dot(k_ref[h, pl.ds(off, blk), :], q)
        s_w[...] = s
        return jnp.max(s, axis=0, keepdims=True)

    def probs(s, cmax):
        m = m_ref[...]
        m_new = jnp.maximum(m, cmax)
        m_ref[...] = m_new
        return jnp.exp2(m - m_new), jnp.exp2(s - m_new).astype(BF16)

    def values(alpha, p, jv):
        acc_ref[...] = alpha * acc_ref[...] + _dot(vt_ref[jv, h * V_ROWS:(h + 1) * V_ROWS, :], p)

    def init():
        p1[...] = jnp.zeros_like(p1)
        m_ref[...] = jnp.full(m_ref.shape, -jnp.inf, F32)
        alpha_ref[...] = jnp.ones_like(alpha_ref)
        acc_ref[...] = jnp.zeros_like(acc_ref)
        cmax_ref[...] = scores(0, s0)

    def stage(j, par):
        s_r, s_w = sbuf[par], sbuf[1 - par]
        p_w, p_r = pbuf[par], pbuf[1 - par]
        alpha_prev, p_prev, cmax, s_cur = alpha_ref[...], p_r[...], cmax_ref[...], s_r[...]
        cmax_ref[...] = scores(j + 1, s_w)
        values(alpha_prev, p_prev, jnp.maximum(j - 1, 0))
        alpha, p = probs(s_cur, cmax)
        alpha_ref[...] = alpha
        p_w[...] = p

    def finish(par):
        s_r, p_r = sbuf[par], pbuf[1 - par]
        values(alpha_ref[...], p_r[...], jnp.maximum(i - 1, 0))
        qc = lax.broadcasted_iota(jnp.int32, (1, blk), 1) // CHUNK
        s = jnp.concatenate(
            [jnp.where(qc >= r, s_r[r * CHUNK:(r + 1) * CHUNK, :], -jnp.inf) for r in range(blk // CHUNK)],
            axis=0)
        alpha, p = probs(s, jnp.max(s, axis=0, keepdims=True))
        values(alpha, p, i)
        out = (acc_ref[0:V_DIM, :] / acc_ref[V_DIM:V_DIM + 1, :]).T
        o_ref[:, h * V_DIM:(h + 1) * V_DIM] = out.astype(o_ref.dtype)

    return init, stage, finish


def _attn_kernel(qt_ref, k_ref, vt_ref, o_ref, *scratch, blk):
    i = pl.program_id(1)
    per = len(scratch) // ATT_HEADS
    heads = [_attn_head(h, i, blk, qt_ref, k_ref, vt_ref, o_ref, *scratch[h * per:(h + 1) * per])
             for h in range(ATT_HEADS)]
    for init, _, _ in heads:
        init()

    def pair(t, c):
        for par in range(2):
            for _, stage, _ in heads:
                stage(2 * t + par, par)
        return c

    lax.fori_loop(0, i // 2, pair, 0)

    @pl.when(i % 2 == 1)
    def _():
        for _, stage, _ in heads:
            stage(i - 1, 0)
        for _, _, finish in heads:
            finish(1)

    @pl.when(i % 2 == 0)
    def _():
        for _, _, finish in heads:
            finish(0)


def _attention(qt, k, vt, blk):
    m = k.shape[1]
    nb = m // blk
    row = pltpu.VMEM((1, blk), F32)
    per_head = ([pltpu.VMEM((blk, blk), F32)] * 2 + [pltpu.VMEM((blk, blk), BF16)] * 2
                + [row, row, row, pltpu.VMEM((V_ROWS, blk), F32)])
    return pl.pallas_call(
        functools.partial(_attn_kernel, blk=blk),
        grid=(MLA_HEADS // ATT_HEADS, nb),
        in_specs=[pl.BlockSpec((ATT_HEADS * QK_PAD, blk), lambda g, i: (g, i)),
                  pl.BlockSpec((ATT_HEADS, m, QK_PAD), lambda g, i: (g, 0, 0)),
                  pl.BlockSpec((nb, ATT_HEADS * V_ROWS, blk), lambda g, i: (0, g, 0), pipeline_mode=pl.Buffered(1))],
        out_specs=pl.BlockSpec((blk, ATT_HEADS * V_DIM), lambda g, i: (i, g)),
        out_shape=jax.ShapeDtypeStruct((m, MLA_HEADS * V_DIM), BF16),
        scratch_shapes=per_head * ATT_HEADS,
        compiler_params=_params("arbitrary", "arbitrary"),
        name="mla_attention",
    )(qt, k, vt)


def _residual_epilogue(acc, x_ref, gpost_ref, gnext_ref, xo_ref, ho_ref):
    xn = x_ref[...] + _rms(acc, gpost_ref[...])
    xo_ref[...] = xn
    if ho_ref is not None:
        ho_ref[...] = _rms(xn, gnext_ref[...]).astype(ho_ref.dtype)


def _mm_res_kernel(a1_ref, a2_ref, w_ref, x_ref, gpost_ref, gnext_ref, xo_ref, *rest, n1, nk, be, bc):
    ho_ref = rest[0] if len(rest) == 2 else None
    acc_ref = rest[-1]
    k = pl.program_id(1)
    n = acc_ref.shape[1]

    def accumulate(a_ref, first):
        a = a_ref[...]
        for c0 in range(0, n, bc):
            d = _dot(a, w_ref[:, c0:c0 + bc])
            if first:
                acc_ref[:, c0:c0 + bc] = d
            else:
                acc_ref[:, c0:c0 + bc] += d

    @pl.when(k == 0)
    def _():
        accumulate(a1_ref, True)

    if n1 > 1:
        @pl.when(jnp.logical_and(k > 0, k < n1))
        def _():
            accumulate(a1_ref, False)

    if n1 < nk:
        @pl.when(jnp.logical_and(k >= n1, k < nk))
        def _():
            accumulate(a2_ref, False)

    @pl.when(k >= nk)
    def _():
        r0 = pl.multiple_of((k - nk) * be, be)
        _residual_epilogue(acc_ref[pl.ds(r0, be), :], x_ref, gpost_ref, gnext_ref, xo_ref, ho_ref)


def _mm_res(a1, a2, w, x, g_post, g_next, l, l_next, bm, bk, be, name):
    m, n = x.shape
    n1 = a1.shape[1] // bk
    nk = w.shape[1] // bk
    ne = bm // be
    if a2 is None:
        a2 = a1
    n2 = max(nk - n1, 1)
    has_next = g_next is not None
    if not has_next:
        g_next, l_next = g_post, l
    rows = lambda i, k: (i * ne + jnp.maximum(k - nk, 0), 0)
    row = pl.BlockSpec((be, n), rows)
    out_specs = [row] + ([row] if has_next else [])
    out_shape = [jax.ShapeDtypeStruct((m, n), F32)] + ([jax.ShapeDtypeStruct((m, n), BF16)] if has_next else [])
    return pl.pallas_call(
        functools.partial(_mm_res_kernel, n1=n1, nk=nk, be=be, bc=512),
        grid=(m // bm, nk + ne),
        in_specs=[pl.BlockSpec((bm, bk), lambda i, k: (i, jnp.minimum(k, n1 - 1))),
                  pl.BlockSpec((bm, bk), lambda i, k: (i, jnp.clip(k - n1, 0, n2 - 1))),
                  pl.BlockSpec((None, bk, n), lambda i, k: (l, jnp.minimum(k, nk - 1), 0)),
                  row, _layer_spec(l, (1, n)), _layer_spec(l_next, (1, n))],
        out_specs=out_specs,
        out_shape=out_shape,
        scratch_shapes=[pltpu.VMEM((bm, n), F32)],
        compiler_params=_params("arbitrary", "arbitrary"),
        name=name,
    )(a1, a2, w, x, g_post, g_next)


def _xattn_kernel(h_ref, x_ref, wq_ref, kv_ref, wo_ref, gpost_ref, gnext_ref, xo_ref, ho_ref, *, br):
    for r0 in range(0, h_ref.shape[0], br):
        rows = slice(r0, r0 + br)
        q = _dot(h_ref[rows, :], wq_ref[...]).astype(BF16)
        outs = []
        for hh in range(X_HEADS):
            c0, c1 = hh * X_HEAD_DIM, (hh + 1) * X_HEAD_DIM
            s = _dot_nt(q[:, c0:c1], kv_ref[:, c0:c1])
            p = jnp.exp(s - jnp.max(s, axis=-1, keepdims=True))
            l = jnp.sum(p, axis=-1, keepdims=True)
            o = _dot(p.astype(BF16), kv_ref[:, X_WIDTH + c0:X_WIDTH + c1]) / l
            outs.append(o.astype(BF16))
        c = _dot(jnp.concatenate(outs, axis=-1), wo_ref[...])
        xn = x_ref[rows, :] + _rms(c, gpost_ref[...])
        xo_ref[rows, :] = xn
        ho_ref[rows, :] = _rms(xn, gnext_ref[...]).astype(ho_ref.dtype)


def _xattn(h, x, w_cq, kv, w_co, g_post, g_next, l, bm=256):
    m, d = x.shape
    row = pl.BlockSpec((bm, d), lambda i: (i, 0))
    return pl.pallas_call(
        functools.partial(_xattn_kernel, br=bm),
        grid=(m // bm,),
        in_specs=[row, row,
                  _layer_spec(l, (d, X_WIDTH)),
                  _const_spec((MEM_LEN, 2 * X_WIDTH)),
                  _layer_spec(l, (X_WIDTH, d)),
                  _layer_spec(l, (1, d)), _layer_spec(l, (1, d))],
        out_specs=[row, row],
        out_shape=[jax.ShapeDtypeStruct((m, d), F32), jax.ShapeDtypeStruct((m, d), BF16)],
        compiler_params=_params("arbitrary"),
        name="mem_xattn",
    )(h, x, w_cq, kv, w_co, g_post, g_next)


def _ffn_up_kernel(h_ref, wg_ref, wu_ref, cw_ref, cb_ref, o_ref, gbuf_ref, carry_ref, *, bm, br):
    i = pl.program_id(0)
    j = pl.program_id(1)

    @pl.when(i == 0)
    def _():
        gbuf_ref[0:CONV_HALO, :] = jnp.zeros((CONV_HALO, gbuf_ref.shape[1]), F32)

    @pl.when(i > 0)
    def _():
        gbuf_ref[0:CONV_HALO, :] = carry_ref[j]

    for r0 in range(0, bm, br):
        h = h_ref[r0:r0 + br, :]
        g = _dot(h, wg_ref[...])
        u = _dot(h, wu_ref[...])
        b0 = CONV_HALO + r0
        gbuf_ref[b0:b0 + br, :] = g
        y = cw_ref[2:3, :] * g + cb_ref[...]
        for tap in range(CONV_W - 1):
            lag = CONV_W - 1 - tap
            y = y + cw_ref[tap:tap + 1, :] * gbuf_ref[b0 - lag:b0 - lag + br, :]
        act = y * (1.0 / (1.0 + jnp.exp(-y))) * u
        o_ref[r0:r0 + br, :] = act.astype(o_ref.dtype)

    carry_ref[j] = gbuf_ref[bm:bm + CONV_HALO, :]


def _ffn_up(h, w_gate, w_up, conv_w, conv_b, l, bm, bn):
    m, d = h.shape
    f = w_gate.shape[-1]
    nj = f // bn
    return pl.pallas_call(
        functools.partial(_ffn_up_kernel, bm=bm, br=256),
        grid=(m // bm, nj),
        in_specs=[pl.BlockSpec((bm, d), lambda i, j: (i, 0)),
                  pl.BlockSpec((None, d, bn), lambda i, j: (l, 0, j)),
                  pl.BlockSpec((None, d, bn), lambda i, j: (l, 0, j)),
                  pl.BlockSpec((None, CONV_W, bn), lambda i, j: (l, 0, j)),
                  pl.BlockSpec((None, 1, bn), lambda i, j: (l, 0, j))],
        out_specs=pl.BlockSpec((bm, bn), lambda i, j: (i, j)),
        out_shape=jax.ShapeDtypeStruct((m, f), BF16),
        scratch_shapes=[pltpu.VMEM((CONV_HALO + bm, bn), F32), pltpu.VMEM((nj, CONV_HALO, bn), F32)],
        compiler_params=_params("arbitrary", "arbitrary"),
        name="ffn_up",
    )(h, w_gate, w_up, conv_w, conv_b)


def _cast_pad_cols_kernel(x_ref, o_ref):
    c = x_ref.shape[-1]
    o_ref[:, 0:c] = x_ref[...].astype(o_ref.dtype)
    o_ref[:, c:] = jnp.zeros((o_ref.shape[0], o_ref.shape[1] - c), o_ref.dtype)


def _cast_pad_cols(w, cols, br=128):
    _, r, c = w.shape
    return pl.pallas_call(
        _cast_pad_cols_kernel,
        grid=(DEPTH, r // br),
        in_specs=[pl.BlockSpec((None, br, c), lambda l, i: (l, i, 0))],
        out_specs=pl.BlockSpec((None, br, cols), lambda l, i: (l, i, 0)),
        out_shape=jax.ShapeDtypeStruct((DEPTH, r, cols), BF16),
        compiler_params=_params("arbitrary", "arbitrary"),
        name="cast_pad_cols",
    )(w)


def _cast_pad_rows_kernel(x_ref, o_ref, *, nvalid):
    i = pl.program_id(1)

    @pl.when(i < nvalid)
    def _():
        o_ref[...] = x_ref[...].astype(o_ref.dtype)

    @pl.when(i >= nvalid)
    def _():
        o_ref[...] = jnp.zeros_like(o_ref)


def _cast_pad_rows(w, rows, br=256):
    _, r, c = w.shape
    nvalid = r // br
    return pl.pallas_call(
        functools.partial(_cast_pad_rows_kernel, nvalid=nvalid),
        grid=(DEPTH, rows // br),
        in_specs=[pl.BlockSpec((None, br, c), lambda l, i: (l, jnp.minimum(i, nvalid - 1), 0))],
        out_specs=pl.BlockSpec((None, br, c), lambda l, i: (l, i, 0)),
        out_shape=jax.ShapeDtypeStruct((DEPTH, rows, c), BF16),
        compiler_params=_params("arbitrary", "arbitrary"),
        name="cast_pad_rows",
    )(w)


def _prep_weights(w_in, w_uq, w_ukv, w_pool, w_out, w_cq, w_ck, w_cv, w_co, w_gate, w_up, conv_w, conv_b, w_down):
    c1 = Q_LORA + KV_LORA
    c2 = c1 + QK_ROPE
    zr = jnp.zeros((DEPTH, D_MODEL, HALF_ROPE), BF16)
    w_in = w_in.astype(BF16)
    w_a = jnp.concatenate([w_in[..., :c1], w_in[..., c1:c1 + HALF_ROPE], zr,
                           w_in[..., c1 + HALF_ROPE:c2], zr], axis=-1)

    qscale = (QK_NOPE + QK_ROPE) ** -0.5 * math.log2(math.e)
    wq = (w_uq * qscale).astype(BF16).reshape(DEPTH, Q_LORA, MLA_HEADS, QK_NOPE + QK_ROPE)
    zq = jnp.zeros((DEPTH, Q_LORA, MLA_HEADS, HALF_ROPE), BF16)
    wq = jnp.concatenate([wq[..., :QK_NOPE], wq[..., QK_NOPE:QK_NOPE + HALF_ROPE], zq,
                          wq[..., QK_NOPE + HALF_ROPE:], zq], axis=-1)
    wkv = w_ukv.astype(BF16).reshape(DEPTH, KV_LORA, MLA_HEADS, QK_NOPE + V_DIM)
    pad_f = D_FF_PAD - D_FF
    return dict(
        w_a=w_a,
        w_u=w_in[..., c2:],
        w_uqt=jnp.swapaxes(wq.reshape(DEPTH, Q_LORA, MLA_HEADS * QK_PAD), 1, 2),
        w_uk=wkv[..., :QK_NOPE].reshape(DEPTH, KV_LORA, MLA_HEADS * QK_NOPE),
        w_uvt=jnp.swapaxes(wkv[..., QK_NOPE:].reshape(DEPTH, KV_LORA, MLA_HEADS * V_DIM), 1, 2),
        w_pool=w_pool.astype(BF16),
        w_out=w_out.astype(BF16),
        w_cq=(w_cq * X_HEAD_DIM ** -0.5).astype(BF16),
        w_ckv=jnp.concatenate([w_ck.astype(BF16), w_cv.astype(BF16)], axis=-1),
        w_co=w_co.astype(BF16),
        w_gate=_cast_pad_cols(w_gate, D_FF_PAD),
        w_up=_cast_pad_cols(w_up, D_FF_PAD),
        conv_w=jnp.pad(conv_w, ((0, 0), (0, 0), (0, pad_f))),
        conv_b=jnp.pad(conv_b, ((0, 0), (0, pad_f))).reshape(DEPTH, 1, D_FF_PAD),
        w_down=_cast_pad_rows(w_down, D_FF_PAD),
    )


def _gain(g):
    return g.reshape(DEPTH, 1, -1)


def kernel(x, mem, positions, g_mix_pre, g_mix_post, w_in, g_q, w_uq, g_kv, w_ukv, w_pool, s_pool, w_out,
           g_x_pre, g_x_post, g_mem, w_cq, w_ck, w_cv, w_co, g_ffn_pre, g_ffn_post, w_gate, w_up,
           conv_w, conv_b, w_down):
    x = x.reshape(SEQ, D_MODEL)
    mem2 = mem.reshape(MEM_LEN, D_MODEL)
    p = _prep_weights(w_in, w_uq, w_ukv, w_pool, w_out, w_cq, w_ck, w_cv, w_co,
                      w_gate, w_up, conv_w, conv_b, w_down)
    g_mix_pre, g_mix_post, g_q, g_kv, s_pool = map(_gain, (g_mix_pre, g_mix_post, g_q, g_kv, s_pool))
    g_x_pre, g_x_post, g_mem, g_ffn_pre, g_ffn_post = map(_gain, (g_x_pre, g_x_post, g_mem, g_ffn_pre, g_ffn_post))
    cosk, sink, cost, sint = _rope_tables(positions)
    h = _norm(x, g_mix_pre, 0)
    for l in range(DEPTH):
        z_a = _mm(h, p["w_a"], l, 512, F32, "in_proj")
        pooled = _pool(h, p["w_u"], p["w_pool"], s_pool, l)
        qt = _qproj(z_a, g_q, p["w_uqt"], cost, sint, l)
        k, vt = _kvproj(z_a, g_kv, p["w_uk"], p["w_uvt"], cosk, sink, l, ATT_BLK)
        att = _attention(qt, k, vt, ATT_BLK)
        x, h = _mm_res(att, pooled, p["w_out"], x, g_mix_post, g_x_pre, l, l, 1024, 1024, 128, "out_proj")
        kv = _norm_mm(mem2, g_mem, p["w_ckv"], l, 512, "mem_kv")
        x, h = _xattn(h, x, p["w_cq"], kv, p["w_co"], g_x_post, g_ffn_pre, l)
        act = _ffn_up(h, p["w_gate"], p["w_up"], p["conv_w"], p["conv_b"], l, 1024, 512)
        last = l + 1 == DEPTH
        res = _mm_res(act, None, p["w_down"], x, g_ffn_post, None if last else g_mix_pre, l, l + 1,
                      1024, 1024, 128, "ffn_down")
        x = res[0]
        h = None if last else res[1]
    return x.reshape(1, SEQ, D_MODEL)
```

```python
import functools
import math

import jax
import jax.numpy as jnp
from jax import lax
from jax.experimental import pallas as pl
from jax.experimental.pallas import tpu as pltpu

F32 = jnp.float32
BF16 = jnp.bfloat16

D_MODEL = 4096
SEQ = 16384
DEPTH = 2
CHUNK = 64
MLA_HEADS = 16
QK_NOPE = 128
QK_ROPE = 64
HALF_ROPE = QK_ROPE // 2
V_DIM = 128
Q_LORA = 1024
KV_LORA = 512
ROPE_THETA = 10000.0
POOL_WINDOWS = (2, 4, 8, 16)
POOL_CH = 2048
POOL_GROUP = 512
POOL_HALO = 16
MEM_LEN = 256
X_HEADS = 4
X_HEAD_DIM = 256
X_WIDTH = X_HEADS * X_HEAD_DIM
D_FF = 11008
D_FF_PAD = 11264
CONV_W = 3
CONV_HALO = 8
EPS = 1e-6

QK_PAD = 256
ROPE_LANES = 128
Z_A = Q_LORA + KV_LORA + ROPE_LANES
ATT_BLK = 512
ATT_HEADS = 2
V_ROWS = V_DIM + 16

V7X_VMEM_LIMIT = 56 * 1024 * 1024


def _params(*sem):
    return pltpu.CompilerParams(dimension_semantics=sem, vmem_limit_bytes=V7X_VMEM_LIMIT)


def _const_spec(shape):
    return pl.BlockSpec(shape, lambda *_: (0,) * len(shape), pipeline_mode=pl.Buffered(1))


def _layer_spec(l, shape):
    return pl.BlockSpec((None,) + tuple(shape), lambda *_: (l,) + (0,) * len(shape),
                        pipeline_mode=pl.Buffered(1))


def _rms(x, g):
    return x * lax.rsqrt(jnp.mean(x * x, axis=-1, keepdims=True) + EPS) * g


def _dot(a, b):
    return jnp.dot(a, b, preferred_element_type=F32)


def _dot_nt(a, b):
    return lax.dot_general(a, b, (((1,), (1,)), ((), ())), preferred_element_type=F32)


def _rope_kernel(pc_ref, pr_ref, invr_ref, sgnr_ref, invc_ref, sgnc_ref,
                 cosk_ref, sink_ref, cost_ref, sint_ref):
    ang = pc_ref[...].astype(F32) * invr_ref[...]
    cosk_ref[...] = jnp.cos(ang)
    sink_ref[...] = jnp.sin(ang) * sgnr_ref[...]
    angt = invc_ref[...] * pr_ref[...].astype(F32)
    cost_ref[...] = jnp.cos(angt)
    sint_ref[...] = jnp.sin(angt) * sgnc_ref[...]


def _rope_tables(positions, bs=2048):
    inv = 1.0 / (ROPE_THETA ** (jnp.arange(0, QK_ROPE, 2, dtype=F32) / QK_ROPE))
    zero = jnp.zeros((HALF_ROPE,), F32)
    one = jnp.ones((HALF_ROPE,), F32)
    inv4 = jnp.concatenate([inv, zero, inv, zero])
    sgn4 = jnp.concatenate([-one, zero, one, zero])
    pc = positions.reshape(SEQ, 1)
    pr = positions.reshape(1, SEQ)
    nat = pl.BlockSpec((bs, ROPE_LANES), lambda i: (i, 0))
    tr = pl.BlockSpec((ROPE_LANES, bs), lambda i: (0, i))
    return pl.pallas_call(
        _rope_kernel,
        grid=(SEQ // bs,),
        in_specs=[pl.BlockSpec((bs, 1), lambda i: (i, 0)),
                  pl.BlockSpec((1, bs), lambda i: (0, i)),
                  _const_spec((1, ROPE_LANES)), _const_spec((1, ROPE_LANES)),
                  _const_spec((ROPE_LANES, 1)), _const_spec((ROPE_LANES, 1))],
        out_specs=[nat, nat, tr, tr],
        out_shape=[jax.ShapeDtypeStruct((SEQ, ROPE_LANES), F32)] * 2
        + [jax.ShapeDtypeStruct((ROPE_LANES, SEQ), F32)] * 2,
        compiler_params=_params("arbitrary"),
        name="rope_tables",
    )(pc, pr, inv4.reshape(1, -1), sgn4.reshape(1, -1), inv4.reshape(-1, 1), sgn4.reshape(-1, 1))


def _norm_kernel(x_ref, g_ref, o_ref):
    o_ref[...] = _rms(x_ref[...], g_ref[...]).astype(o_ref.dtype)


def _norm(x, g, l, bm=512):
    m, d = x.shape
    return pl.pallas_call(
        _norm_kernel,
        grid=(m // bm,),
        in_specs=[pl.BlockSpec((bm, d), lambda i: (i, 0)), _layer_spec(l, (1, d))],
        out_specs=pl.BlockSpec((bm, d), lambda i: (i, 0)),
        out_shape=jax.ShapeDtypeStruct((m, d), BF16),
        compiler_params=_params("arbitrary"),
        name="norm",
    )(x, g)


def _mm_kernel(a_ref, w_ref, o_ref):
    o_ref[...] = _dot(a_ref[...], w_ref[...]).astype(o_ref.dtype)


def _mm(a, w, l, bm, out_dtype, name):
    m, k = a.shape
    n = w.shape[-1]
    return pl.pallas_call(
        _mm_kernel,
        grid=(m // bm,),
        in_specs=[pl.BlockSpec((bm, k), lambda i: (i, 0)), _layer_spec(l, (k, n))],
        out_specs=pl.BlockSpec((bm, n), lambda i: (i, 0)),
        out_shape=jax.ShapeDtypeStruct((m, n), out_dtype),
        compiler_params=_params("arbitrary"),
        name=name,
    )(a, w)


def _norm_mm_kernel(a_ref, g_ref, w_ref, o_ref):
    a = _rms(a_ref[...], g_ref[...]).astype(BF16)
    o_ref[...] = _dot(a, w_ref[...]).astype(o_ref.dtype)


def _norm_mm(a, g, w, l, bn, name):
    m, k = a.shape
    n = w.shape[-1]
    return pl.pallas_call(
        _norm_mm_kernel,
        grid=(n // bn,),
        in_specs=[_const_spec((m, k)), _layer_spec(l, (1, k)),
                  pl.BlockSpec((None, k, bn), lambda j: (l, 0, j))],
        out_specs=pl.BlockSpec((m, bn), lambda j: (0, j)),
        out_shape=jax.ShapeDtypeStruct((m, n), BF16),
        compiler_params=_params("arbitrary"),
        name=name,
    )(a, g, w)


def _pool_kernel(h_ref, wu_ref, wp_ref, sp_ref, o_ref, buf_ref, carry_ref, *, bm):
    i = pl.program_id(0)

    @pl.when(i == 0)
    def _():
        buf_ref[0:POOL_HALO, :] = jnp.zeros((POOL_HALO, POOL_CH), F32)

    @pl.when(i > 0)
    def _():
        buf_ref[0:POOL_HALO, :] = carry_ref[...]

    h = h_ref[...]
    t1 = i * bm + lax.broadcasted_iota(jnp.int32, (bm, 1), 0) + 1
    for g, w in reversed(tuple(enumerate(POOL_WINDOWS))):
        c0, c1 = g * POOL_GROUP, (g + 1) * POOL_GROUP
        ug = _dot(h, wu_ref[:, c0:c1])
        buf_ref[POOL_HALO:POOL_HALO + bm, c0:c1] = ug
        carry_ref[:, c0:c1] = ug[bm - POOL_HALO:bm, :]
        s = ug
        for k in range(1, w):
            s = s + buf_ref[POOL_HALO - k:POOL_HALO - k + bm, c0:c1]
        inv_cnt = 1.0 / jnp.minimum(t1, w).astype(F32)
        p = (s * inv_cnt - ug).astype(BF16)
        y = _dot(p, wp_ref[g]) * sp_ref[:, c0:c1]
        o_ref[:, c0:c1] = y.astype(o_ref.dtype)


def _pool(h, w_u, w_pool, s_pool, l, bm=512):
    m, d = h.shape
    return pl.pallas_call(
        functools.partial(_pool_kernel, bm=bm),
        grid=(m // bm,),
        in_specs=[pl.BlockSpec((bm, d), lambda i: (i, 0)),
                  _layer_spec(l, (d, POOL_CH)),
                  _layer_spec(l, (len(POOL_WINDOWS), POOL_GROUP, POOL_GROUP)),
                  _layer_spec(l, (1, POOL_CH))],
        out_specs=pl.BlockSpec((bm, POOL_CH), lambda i: (i, 0)),
        out_shape=jax.ShapeDtypeStruct((m, POOL_CH), BF16),
        scratch_shapes=[pltpu.VMEM((POOL_HALO + bm, POOL_CH), F32), pltpu.VMEM((POOL_HALO, POOL_CH), F32)],
        compiler_params=_params("arbitrary"),
        name="pool",
    )(h, w_u, w_pool, s_pool)


def _qproj_kernel(cq_ref, g_ref, w_ref, cos_ref, sin_ref, o_ref):
    cqn = _rms(cq_ref[...], g_ref[...]).astype(BF16)
    qt = _dot_nt(w_ref[...], cqn)
    cos = cos_ref[...]
    sin = sin_ref[...]
    half = ROPE_LANES // 2
    for h in range(MLA_HEADS):
        r0 = h * QK_PAD
        o_ref[r0:r0 + QK_NOPE, :] = qt[r0:r0 + QK_NOPE, :].astype(o_ref.dtype)
        blk = qt[r0 + QK_NOPE:r0 + QK_PAD, :]
        swp = jnp.concatenate([blk[half:, :], blk[:half, :]], axis=0)
        o_ref[r0 + QK_NOPE:r0 + QK_PAD, :] = (blk * cos + swp * sin).astype(o_ref.dtype)


def _qproj(z_a, g_q, w_uqt, cost, sint, l, bm=512):
    m = z_a.shape[0]
    rows = MLA_HEADS * QK_PAD
    return pl.pallas_call(
        _qproj_kernel,
        grid=(m // bm,),
        in_specs=[pl.BlockSpec((bm, Q_LORA), lambda i: (i, 0)),
                  _layer_spec(l, (1, Q_LORA)),
                  _layer_spec(l, (rows, Q_LORA)),
                  pl.BlockSpec((ROPE_LANES, bm), lambda i: (0, i)),
                  pl.BlockSpec((ROPE_LANES, bm), lambda i: (0, i))],
        out_specs=pl.BlockSpec((rows, bm), lambda i: (0, i)),
        out_shape=jax.ShapeDtypeStruct((rows, m), BF16),
        compiler_params=_params("arbitrary"),
        name="q_proj",
    )(z_a, g_q, w_uqt, cost, sint)


def _kvproj_kernel(ckv_ref, kr_ref, g_ref, wk_ref, wvt_ref, cos_ref, sin_ref, k_ref, vt_ref):
    ckvn = _rms(ckv_ref[...], g_ref[...]).astype(BF16)
    kn = _dot(ckvn, wk_ref[...])
    vt = _dot_nt(wvt_ref[...], ckvn)
    ones_row = (lax.broadcasted_iota(jnp.int32, (V_ROWS - V_DIM, vt.shape[1]), 0) == 0).astype(vt_ref.dtype)
    for h in range(MLA_HEADS):
        vt_ref[0, h * V_ROWS:h * V_ROWS + V_DIM, :] = vt[h * V_DIM:(h + 1) * V_DIM, :].astype(vt_ref.dtype)
        vt_ref[0, h * V_ROWS + V_DIM:(h + 1) * V_ROWS, :] = ones_row
    kr = kr_ref[...]
    swp = pltpu.roll(kr, ROPE_LANES // 2, 1)
    krope = (kr * cos_ref[...] + swp * sin_ref[...]).astype(k_ref.dtype)
    for h in range(MLA_HEADS):
        k_ref[h, :, 0:QK_NOPE] = kn[:, h * QK_NOPE:(h + 1) * QK_NOPE].astype(k_ref.dtype)
        k_ref[h, :, QK_NOPE:QK_PAD] = krope


def _kvproj(z_a, g_kv, w_uk, w_uvt, cosk, sink, l, bm):
    m = z_a.shape[0]
    return pl.pallas_call(
        _kvproj_kernel,
        grid=(m // bm,),
        in_specs=[pl.BlockSpec((bm, KV_LORA), lambda i: (i, Q_LORA // KV_LORA)),
                  pl.BlockSpec((bm, ROPE_LANES), lambda i: (i, (Q_LORA + KV_LORA) // ROPE_LANES)),
                  _layer_spec(l, (1, KV_LORA)),
                  _layer_spec(l, (KV_LORA, MLA_HEADS * QK_NOPE)),
                  _layer_spec(l, (MLA_HEADS * V_DIM, KV_LORA)),
                  pl.BlockSpec((bm, ROPE_LANES), lambda i: (i, 0)),
                  pl.BlockSpec((bm, ROPE_LANES), lambda i: (i, 0))],
        out_specs=[pl.BlockSpec((MLA_HEADS, bm, QK_PAD), lambda i: (0, i, 0)),
                   pl.BlockSpec((1, MLA_HEADS * V_ROWS, bm), lambda i: (i, 0, 0))],
        out_shape=[jax.ShapeDtypeStruct((MLA_HEADS, m, QK_PAD), BF16),
                   jax.ShapeDtypeStruct((m // bm, MLA_HEADS * V_ROWS, bm), BF16)],
        compiler_params=_params("arbitrary"),
        name="kv_proj",
    )(z_a, z_a, g_kv, w_uk, w_uvt, cosk, sink)


def _attn_head(h, i, blk, qt_ref, k_ref, vt_ref, o_ref, s0, s1, p0, p1, m_ref, alpha_ref, cmax_ref, acc_ref):
    q = qt_ref[h * QK_PAD:(h + 1) * QK_PAD, :]
    sbuf = (s0, s1)
    pbuf = (p0, p1)

    def scores(jn, s_w):
        off = pl.multiple_of(jn * blk, blk)
        s = _dot(k_ref[h, pl.ds(off, blk), :], q)
        s_w[...] = s
        return jnp.max(s, axis=0, keepdims=True)

    def probs(s, cmax):
        m = m_ref[...]
        m_new = jnp.maximum(m, cmax)
        m_ref[...] = m_new
        return jnp.exp2(m - m_new), jnp.exp2(s - m_new).astype(BF16)

    def values(alpha, p, jv):
        acc_ref[...] = alpha * acc_ref[...] + _dot(vt_ref[jv, h * V_ROWS:(h + 1) * V_ROWS, :], p)

    def init():
        p1[...] = jnp.zeros_like(p1)
        m_ref[...] = jnp.full(m_ref.shape, -jnp.inf, F32)
        alpha_ref[...] = jnp.ones_like(alpha_ref)
        acc_ref[...] = jnp.zeros_like(acc_ref)
        cmax_ref[...] = scores(0, s0)

    def stage(j, par):
        s_r, s_w = sbuf[par], sbuf[1 - par]
        p_w, p_r = pbuf[par], pbuf[1 - par]
        alpha_prev, p_prev, cmax, s_cur = alpha_ref[...], p_r[...], cmax_ref[...], s_r[...]
        cmax_ref[...] = scores(j + 1, s_w)
        values(alpha_prev, p_prev, jnp.maximum(j - 1, 0))
        alpha, p = probs(s_cur, cmax)
        alpha_ref[...] = alpha
        p_w[...] = p

    def finish(par):
        s_r, p_r = sbuf[par], pbuf[1 - par]
        values(alpha_ref[...], p_r[...], jnp.maximum(i - 1, 0))
        qc = lax.broadcasted_iota(jnp.int32, (1, blk), 1) // CHUNK
        s = jnp.concatenate(
            [jnp.where(qc >= r, s_r[r * CHUNK:(r + 1) * CHUNK, :], -jnp.inf) for r in range(blk // CHUNK)],
            axis=0)
        alpha, p = probs(s, jnp.max(s, axis=0, keepdims=True))
        values(alpha, p, i)
        out = (acc_ref[0:V_DIM, :] / acc_ref[V_DIM:V_DIM + 1, :]).T
        o_ref[:, h * V_DIM:(h + 1) * V_DIM] = out.astype(o_ref.dtype)

    return init, stage, finish


def _attn_kernel(qt_ref, k_ref, vt_ref, o_ref, *scratch, blk):
    i = pl.program_id(1)
    per = len(scratch) // ATT_HEADS
    heads = [_attn_head(h, i, blk, qt_ref, k_ref, vt_ref, o_ref, *scratch[h * per:(h + 1) * per])
             for h in range(ATT_HEADS)]
    for init, _, _ in heads:
        init()

    def pair(t, c):
        for par in range(2):
            for _, stage, _ in heads:
                stage(2 * t + par, par)
        return c

    lax.fori_loop(0, i // 2, pair, 0)

    @pl.when(i % 2 == 1)
    def _():
        for _, stage, _ in heads:
            stage(i - 1, 0)
        for _, _, finish in heads:
            finish(1)

    @pl.when(i % 2 == 0)
    def _():
        for _, _, finish in heads:
            finish(0)


def _attention(qt, k, vt, blk):
    m = k.shape[1]
    nb = m // blk
    row = pltpu.VMEM((1, blk), F32)
    per_head = ([pltpu.VMEM((blk, blk), F32)] * 2 + [pltpu.VMEM((blk, blk), BF16)] * 2
                + [row, row, row, pltpu.VMEM((V_ROWS, blk), F32)])
    return pl.pallas_call(
        functools.partial(_attn_kernel, blk=blk),
        grid=(MLA_HEADS // ATT_HEADS, nb),
        in_specs=[pl.BlockSpec((ATT_HEADS * QK_PAD, blk), lambda g, i: (g, i)),
                  pl.BlockSpec((ATT_HEADS, m, QK_PAD), lambda g, i: (g, 0, 0)),
                  pl.BlockSpec((nb, ATT_HEADS * V_ROWS, blk), lambda g, i: (0, g, 0), pipeline_mode=pl.Buffered(1))],
        out_specs=pl.BlockSpec((blk, ATT_HEADS * V_DIM), lambda g, i: (i, g)),
        out_shape=jax.ShapeDtypeStruct((m, MLA_HEADS * V_DIM), BF16),
        scratch_shapes=per_head * ATT_HEADS,
        compiler_params=_params("arbitrary", "arbitrary"),
        name="mla_attention",
    )(qt, k, vt)


def _residual_epilogue(acc, x_ref, gpost_ref, gnext_ref, xo_ref, ho_ref):
    xn = x_ref[...] + _rms(acc, gpost_ref[...])
    xo_ref[...] = xn
    if ho_ref is not None:
        ho_ref[...] = _rms(xn, gnext_ref[...]).astype(ho_ref.dtype)


def _mm_res_kernel(a1_ref, a2_ref, w_ref, x_ref, gpost_ref, gnext_ref, xo_ref, *rest, n1, nk, be, bc):
    ho_ref = rest[0] if len(rest) == 2 else None
    acc_ref = rest[-1]
    k = pl.program_id(1)
    n = acc_ref.shape[1]

    def accumulate(a_ref, first):
        a = a_ref[...]
        for c0 in range(0, n, bc):
            d = _dot(a, w_ref[:, c0:c0 + bc])
            if first:
                acc_ref[:, c0:c0 + bc] = d
            else:
                acc_ref[:, c0:c0 + bc] += d

    @pl.when(k == 0)
    def _():
        accumulate(a1_ref, True)

    if n1 > 1:
        @pl.when(jnp.logical_and(k > 0, k < n1))
        def _():
            accumulate(a1_ref, False)

    if n1 < nk:
        @pl.when(jnp.logical_and(k >= n1, k < nk))
        def _():
            accumulate(a2_ref, False)

    @pl.when(k >= nk)
    def _():
        r0 = pl.multiple_of((k - nk) * be, be)
        _residual_epilogue(acc_ref[pl.ds(r0, be), :], x_ref, gpost_ref, gnext_ref, xo_ref, ho_ref)


def _mm_res(a1, a2, w, x, g_post, g_next, l, l_next, bm, bk, be, name):
    m, n = x.shape
    n1 = a1.shape[1] // bk
    nk = w.shape[1] // bk
    ne = bm // be
    if a2 is None:
        a2 = a1
    n2 = max(nk - n1, 1)
    has_next = g_next is not None
    if not has_next:
        g_next, l_next = g_post, l
    rows = lambda i, k: (i * ne + jnp.maximum(k - nk, 0), 0)
    row = pl.BlockSpec((be, n), rows)
    out_specs = [row] + ([row] if has_next else [])
    out_shape = [jax.ShapeDtypeStruct((m, n), F32)] + ([jax.ShapeDtypeStruct((m, n), BF16)] if has_next else [])
    return pl.pallas_call(
        functools.partial(_mm_res_kernel, n1=n1, nk=nk, be=be, bc=512),
        grid=(m // bm, nk + ne),
        in_specs=[pl.BlockSpec((bm, bk), lambda i, k: (i, jnp.minimum(k, n1 - 1))),
                  pl.BlockSpec((bm, bk), lambda i, k: (i, jnp.clip(k - n1, 0, n2 - 1))),
                  pl.BlockSpec((None, bk, n), lambda i, k: (l, jnp.minimum(k, nk - 1), 0)),
                  row, _layer_spec(l, (1, n)), _layer_spec(l_next, (1, n))],
        out_specs=out_specs,
        out_shape=out_shape,
        scratch_shapes=[pltpu.VMEM((bm, n), F32)],
        compiler_params=_params("arbitrary", "arbitrary"),
        name=name,
    )(a1, a2, w, x, g_post, g_next)


def _xattn_kernel(q_ref, x_ref, kv_ref, wo_ref, gpost_ref, gnext_ref, xo_ref, ho_ref, *, br):
    for r0 in range(0, q_ref.shape[0], br):
        rows = slice(r0, r0 + br)
        outs = []
        for hh in range(X_HEADS):
            c0, c1 = hh * X_HEAD_DIM, (hh + 1) * X_HEAD_DIM
            s = _dot_nt(q_ref[rows, c0:c1], kv_ref[:, c0:c1])
            p = jnp.exp(s - jnp.max(s, axis=-1, keepdims=True))
            l = jnp.sum(p, axis=-1, keepdims=True)
            o = _dot(p.astype(BF16), kv_ref[:, X_WIDTH + c0:X_WIDTH + c1]) / l
            outs.append(o.astype(BF16))
        c = _dot(jnp.concatenate(outs, axis=-1), wo_ref[...])
        xn = x_ref[rows, :] + _rms(c, gpost_ref[...])
        xo_ref[rows, :] = xn
        ho_ref[rows, :] = _rms(xn, gnext_ref[...]).astype(ho_ref.dtype)


def _xattn(q, x, kv, w_co, g_post, g_next, l, bm=512, br=256):
    m, d = x.shape
    row = pl.BlockSpec((bm, d), lambda i: (i, 0))
    return pl.pallas_call(
        functools.partial(_xattn_kernel, br=br),
        grid=(m // bm,),
        in_specs=[pl.BlockSpec((bm, X_WIDTH), lambda i: (i, 0)), row,
                  _const_spec((MEM_LEN, 2 * X_WIDTH)),
                  _layer_spec(l, (X_WIDTH, d)),
                  _layer_spec(l, (1, d)), _layer_spec(l, (1, d))],
        out_specs=[row, row],
        out_shape=[jax.ShapeDtypeStruct((m, d), F32), jax.ShapeDtypeStruct((m, d), BF16)],
        compiler_params=_params("arbitrary"),
        name="mem_xattn",
    )(q, x, kv, w_co, g_post, g_next)


def _ffn_up_kernel(h_ref, wg_ref, wu_ref, cw_ref, cb_ref, o_ref, gbuf_ref, carry_ref, *, bm, br):
    i = pl.program_id(0)
    j = pl.program_id(1)

    @pl.when(i == 0)
    def _():
        gbuf_ref[0:CONV_HALO, :] = jnp.zeros((CONV_HALO, gbuf_ref.shape[1]), F32)

    @pl.when(i > 0)
    def _():
        gbuf_ref[0:CONV_HALO, :] = carry_ref[j]

    for r0 in range(0, bm, br):
        h = h_ref[r0:r0 + br, :]
        g = _dot(h, wg_ref[...])
        u = _dot(h, wu_ref[...])
        b0 = CONV_HALO + r0
        gbuf_ref[b0:b0 + br, :] = g
        y = cw_ref[2:3, :] * g + cb_ref[...]
        for tap in range(CONV_W - 1):
            lag = CONV_W - 1 - tap
            y = y + cw_ref[tap:tap + 1, :] * gbuf_ref[b0 - lag:b0 - lag + br, :]
        act = y * (1.0 / (1.0 + jnp.exp(-y))) * u
        o_ref[r0:r0 + br, :] = act.astype(o_ref.dtype)

    carry_ref[j] = gbuf_ref[bm:bm + CONV_HALO, :]


def _ffn_up(h, w_gate, w_up, conv_w, conv_b, l, bm, bn):
    m, d = h.shape
    f = w_gate.shape[-1]
    nj = f // bn
    return pl.pallas_call(
        functools.partial(_ffn_up_kernel, bm=bm, br=256),
        grid=(m // bm, nj),
        in_specs=[pl.BlockSpec((bm, d), lambda i, j: (i, 0)),
                  pl.BlockSpec((None, d, bn), lambda i, j: (l, 0, j)),
                  pl.BlockSpec((None, d, bn), lambda i, j: (l, 0, j)),
                  pl.BlockSpec((None, CONV_W, bn), lambda i, j: (l, 0, j)),
                  pl.BlockSpec((None, 1, bn), lambda i, j: (l, 0, j))],
        out_specs=pl.BlockSpec((bm, bn), lambda i, j: (i, j)),
        out_shape=jax.ShapeDtypeStruct((m, f), BF16),
        scratch_shapes=[pltpu.VMEM((CONV_HALO + bm, bn), F32), pltpu.VMEM((nj, CONV_HALO, bn), F32)],
        compiler_params=_params("arbitrary", "arbitrary"),
        name="ffn_up",
    )(h, w_gate, w_up, conv_w, conv_b)


def _cast_pad_cols_kernel(x_ref, o_ref):
    c = x_ref.shape[-1]
    o_ref[:, 0:c] = x_ref[...].astype(o_ref.dtype)
    o_ref[:, c:] = jnp.zeros((o_ref.shape[0], o_ref.shape[1] - c), o_ref.dtype)


def _cast_pad_cols(w, cols, br=128):
    _, r, c = w.shape
    return pl.pallas_call(
        _cast_pad_cols_kernel,
        grid=(DEPTH, r // br),
        in_specs=[pl.BlockSpec((None, br, c), lambda l, i: (l, i, 0))],
        out_specs=pl.BlockSpec((None, br, cols), lambda l, i: (l, i, 0)),
        out_shape=jax.ShapeDtypeStruct((DEPTH, r, cols), BF16),
        compiler_params=_params("arbitrary", "arbitrary"),
        name="cast_pad_cols",
    )(w)


def _cast_pad_rows_kernel(x_ref, o_ref, *, nvalid):
    i = pl.program_id(1)

    @pl.when(i < nvalid)
    def _():
        o_ref[...] = x_ref[...].astype(o_ref.dtype)

    @pl.when(i >= nvalid)
    def _():
        o_ref[...] = jnp.zeros_like(o_ref)


def _cast_pad_rows(w, rows, br=256):
    _, r, c = w.shape
    nvalid = r // br
    return pl.pallas_call(
        functools.partial(_cast_pad_rows_kernel, nvalid=nvalid),
        grid=(DEPTH, rows // br),
        in_specs=[pl.BlockSpec((None, br, c), lambda l, i: (l, jnp.minimum(i, nvalid - 1), 0))],
        out_specs=pl.BlockSpec((None, br, c), lambda l, i: (l, i, 0)),
        out_shape=jax.ShapeDtypeStruct((DEPTH, rows, c), BF16),
        compiler_params=_params("arbitrary", "arbitrary"),
        name="cast_pad_rows",
    )(w)


def _prep_weights(w_in, w_uq, w_ukv, w_pool, w_out, w_cq, w_ck, w_cv, w_co, w_gate, w_up, conv_w, conv_b, w_down):
    c1 = Q_LORA + KV_LORA
    c2 = c1 + QK_ROPE
    zr = jnp.zeros((DEPTH, D_MODEL, HALF_ROPE), BF16)
    w_in = w_in.astype(BF16)
    w_a = jnp.concatenate([w_in[..., :c1], w_in[..., c1:c1 + HALF_ROPE], zr,
                           w_in[..., c1 + HALF_ROPE:c2], zr], axis=-1)

    qscale = (QK_NOPE + QK_ROPE) ** -0.5 * math.log2(math.e)
    wq = (w_uq * qscale).astype(BF16).reshape(DEPTH, Q_LORA, MLA_HEADS, QK_NOPE + QK_ROPE)
    zq = jnp.zeros((DEPTH, Q_LORA, MLA_HEADS, HALF_ROPE), BF16)
    wq = jnp.concatenate([wq[..., :QK_NOPE], wq[..., QK_NOPE:QK_NOPE + HALF_ROPE], zq,
                          wq[..., QK_NOPE + HALF_ROPE:], zq], axis=-1)
    wkv = w_ukv.astype(BF16).reshape(DEPTH, KV_LORA, MLA_HEADS, QK_NOPE + V_DIM)
    pad_f = D_FF_PAD - D_FF
    return dict(
        w_a=w_a,
        w_u=w_in[..., c2:],
        w_uqt=jnp.swapaxes(wq.reshape(DEPTH, Q_LORA, MLA_HEADS * QK_PAD), 1, 2),
        w_uk=wkv[..., :QK_NOPE].reshape(DEPTH, KV_LORA, MLA_HEADS * QK_NOPE),
        w_uvt=jnp.swapaxes(wkv[..., QK_NOPE:].reshape(DEPTH, KV_LORA, MLA_HEADS * V_DIM), 1, 2),
        w_pool=w_pool.astype(BF16),
        w_out=w_out.astype(BF16),
        w_cq=(w_cq * X_HEAD_DIM ** -0.5).astype(BF16),
        w_ckv=jnp.concatenate([w_ck.astype(BF16), w_cv.astype(BF16)], axis=-1),
        w_co=w_co.astype(BF16),
        w_gate=_cast_pad_cols(w_gate, D_FF_PAD),
        w_up=_cast_pad_cols(w_up, D_FF_PAD),
        conv_w=jnp.pad(conv_w, ((0, 0), (0, 0), (0, pad_f))),
        conv_b=jnp.pad(conv_b, ((0, 0), (0, pad_f))).reshape(DEPTH, 1, D_FF_PAD),
        w_down=_cast_pad_rows(w_down, D_FF_PAD),
    )


def _gain(g):
    return g.reshape(DEPTH, 1, -1)


def kernel(x, mem, positions, g_mix_pre, g_mix_post, w_in, g_q, w_uq, g_kv, w_ukv, w_pool, s_pool, w_out,
           g_x_pre, g_x_post, g_mem, w_cq, w_ck, w_cv, w_co, g_ffn_pre, g_ffn_post, w_gate, w_up,
           conv_w, conv_b, w_down):
    x = x.reshape(SEQ, D_MODEL)
    mem2 = mem.reshape(MEM_LEN, D_MODEL)
    p = _prep_weights(w_in, w_uq, w_ukv, w_pool, w_out, w_cq, w_ck, w_cv, w_co,
                      w_gate, w_up, conv_w, conv_b, w_down)
    g_mix_pre, g_mix_post, g_q, g_kv, s_pool = map(_gain, (g_mix_pre, g_mix_post, g_q, g_kv, s_pool))
    g_x_pre, g_x_post, g_mem, g_ffn_pre, g_ffn_post = map(_gain, (g_x_pre, g_x_post, g_mem, g_ffn_pre, g_ffn_post))
    cosk, sink, cost, sint = _rope_tables(positions)
    h = _norm(x, g_mix_pre, 0)
    for l in range(DEPTH):
        z_a = _mm(h, p["w_a"], l, 512, F32, "in_proj")
        pooled = _pool(h, p["w_u"], p["w_pool"], s_pool, l)
        qt = _qproj(z_a, g_q, p["w_uqt"], cost, sint, l)
        k, vt = _kvproj(z_a, g_kv, p["w_uk"], p["w_uvt"], cosk, sink, l, ATT_BLK)
        att = _attention(qt, k, vt, ATT_BLK)
        x, h = _mm_res(att, pooled, p["w_out"], x, g_mix_post, g_x_pre, l, l, 1024, 1024, 128, "out_proj")
        kv = _norm_mm(mem2, g_mem, p["w_ckv"], l, 512, "mem_kv")
        xq = _mm(h, p["w_cq"], l, 512, BF16, "xq_proj")
        x, h = _xattn(xq, x, kv, p["w_co"], g_x_post, g_ffn_pre, l)
        act = _ffn_up(h, p["w_gate"], p["w_up"], p["conv_w"], p["conv_b"], l, 1024, 512)
        last = l + 1 == DEPTH
        res = _mm_res(act, None, p["w_down"], x, g_ffn_post, None if last else g_mix_pre, l, l + 1,
                      1024, 1024, 128, "ffn_down")
        x = res[0]
        h = None if last else res[1]
    return x.reshape(1, SEQ, D_MODEL)
```
